```python
import math
import jax, jax.numpy as jnp
from jax import lax
import numpy as np

D_MODEL = 1024
BATCH = 8
SEQ = 8192
DEPTH = 2

N_MIXERS = 2
DA_HEAD_DIM = 64
DA_HEADS = D_MODEL // (2 * DA_HEAD_DIM)
DA_QK_WIDTH = 2 * DA_HEADS * DA_HEAD_DIM
DA_V_WIDTH = DA_HEADS * 2 * DA_HEAD_DIM
Q_BLOCK = 128
ROPE_THETA = 10000.0
SG_CHUNK = 128
SG_GROUPS = 8
SG_HALF = 2 * D_MODEL
SG_GROUP_DIM = SG_HALF // SG_GROUPS
N_EXPERTS = 64
TOP_K = 6
N_EXPERT_GROUPS = 8
TOPK_GROUPS = 4
EXPERT_DIM = D_MODEL // 4
SHARED_DIM = EXPERT_DIM
ROUTED_SCALE = 2.5
MOE_BLOCK = 128
EPS = 1e-6

N_DA_LAYERS = (DEPTH + 1) // 2
N_SG_LAYERS = DEPTH // 2

kernel_name = "hybrid_diffattn_chunkgmlp_moe_adaln"


def rms_norm(x, g):
    xf = x.astype(jnp.float32)
    y = xf * lax.rsqrt(jnp.mean(xf * xf, axis=-1, keepdims=True) + EPS)
    return (y * g.astype(jnp.float32)).astype(x.dtype)


def layer_norm(x, g, b):
    xf = x.astype(jnp.float32)
    mu = jnp.mean(xf, axis=-1, keepdims=True)
    xc = xf - mu
    y = xc * lax.rsqrt(jnp.mean(xc * xc, axis=-1, keepdims=True) + EPS)
    return (y * g.astype(jnp.float32) + b.astype(jnp.float32)).astype(x.dtype)


def modulate(h, shift, scale):
    return h * (1 + scale[:, None, :]) + shift[:, None, :]


def apply_rope(x, cos, sin):
    x1, x2 = jnp.split(x, 2, axis=-1)
    c = cos[:, :, None, :]
    s = sin[:, :, None, :]
    return jnp.concatenate([x1 * c - x2 * s, x2 * c + x1 * s], axis=-1)


def diff_attention(h, positions, w_in, lam_q1, lam_k1, lam_q2, lam_k2, subln_g, w_out, lambda_init):
    B, S, _ = h.shape
    H, d = DA_HEADS, DA_HEAD_DIM
    qkv = h @ w_in
    q, k, v = jnp.split(qkv, [DA_QK_WIDTH, 2 * DA_QK_WIDTH], axis=-1)
    q = q.reshape(B, S, 2 * H, d)
    k = k.reshape(B, S, 2 * H, d)
    v = v.reshape(B, S, H, 2 * d)
    inv_freq = ROPE_THETA ** (-jnp.arange(0, d, 2, dtype=jnp.float32) / d)
    ang = positions.astype(jnp.float32)[..., None] * inv_freq
    cos = jnp.cos(ang).astype(h.dtype)
    sin = jnp.sin(ang).astype(h.dtype)
    q = apply_rope(q, cos, sin) * (d ** -0.5)
    k = apply_rope(k, cos, sin)
    f32 = jnp.float32
    lam = (jnp.exp(jnp.sum(lam_q1.astype(f32) * lam_k1.astype(f32)))
           - jnp.exp(jnp.sum(lam_q2.astype(f32) * lam_k2.astype(f32))) + lambda_init)
    nqb = S // Q_BLOCK
    qb = q.reshape(B, nqb, Q_BLOCK, 2 * H, d).transpose(1, 0, 2, 3, 4)

    def block(q_blk):
        s = jnp.einsum('bqhd,bkhd->bhqk', q_blk, k, preferred_element_type=jnp.float32)
        p = jax.nn.softmax(s, axis=-1).reshape(B, H, 2, Q_BLOCK, S)
        a = (p[:, :, 0] - lam * p[:, :, 1]).astype(v.dtype)
        return jnp.einsum('bhqk,bkhe->bqhe', a, v)

    o = lax.map(block, qb)
    o = o.transpose(1, 0, 2, 3, 4).reshape(B, S, H, 2 * d)
    o = rms_norm(o, subln_g) * (1.0 - lambda_init)
    return o.reshape(B, S, H * 2 * d) @ w_out


def chunked_spatial_gating(h, w_in, ln_g, ln_b, w_s, b_s, w_out):
    B, S, _ = h.shape
    z = jax.nn.gelu(h @ w_in, approximate=False)
    u, v = jnp.split(z, 2, axis=-1)
    v = layer_norm(v, ln_g, ln_b)
    n = S // SG_CHUNK
    v = v.reshape(B, n, SG_CHUNK, SG_GROUPS, SG_GROUP_DIM)
    s = jnp.einsum('gpq,bnqgc->bnpgc', w_s, v) + b_s.T[None, None, :, :, None]
    return (u * s.reshape(B, S, SG_HALF)) @ w_out


def moe(h, w_router, router_bias, w_gate, w_up, w_down, ws_gate, ws_up, ws_down):
    B, S, D = h.shape
    T = B * S
    xf = h.reshape(T, D)
    logits = jnp.dot(xf, w_router, preferred_element_type=jnp.float32)
    scores = jax.nn.sigmoid(logits)
    sel = scores + router_bias.astype(jnp.float32)
    grp = sel.reshape(T, N_EXPERT_GROUPS, N_EXPERTS // N_EXPERT_GROUPS)
    grp_score = jnp.sum(lax.top_k(grp, 2)[0], axis=-1)
    _, gidx = lax.top_k(grp_score, TOPK_GROUPS)
    gmask = jnp.sum(jax.nn.one_hot(gidx, N_EXPERT_GROUPS, dtype=jnp.float32), axis=1) > 0
    emask = jnp.repeat(gmask, N_EXPERTS // N_EXPERT_GROUPS, axis=1)
    _, eidx = lax.top_k(jnp.where(emask, sel, -jnp.inf), TOP_K)
    wts = jnp.take_along_axis(scores, eidx, axis=1)
    wts = wts / jnp.sum(wts, axis=-1, keepdims=True) * ROUTED_SCALE
    TK = T * TOP_K
    flat_e = eidx.reshape(TK)
    flat_tok = jnp.arange(TK, dtype=jnp.int32) // TOP_K
    flat_w = wts.reshape(TK).astype(h.dtype)
    order = jnp.argsort(flat_e)
    se = flat_e[order]
    counts = jnp.bincount(flat_e, length=N_EXPERTS)
    padded = (counts + MOE_BLOCK - 1) // MOE_BLOCK * MOE_BLOCK
    pad_end = jnp.cumsum(padded)
    pad_start = pad_end - padded
    start = jnp.cumsum(counts) - counts
    dest = pad_start[se] + jnp.arange(TK, dtype=jnp.int32) - start[se]
    nb = TK // MOE_BLOCK + N_EXPERTS
    P = nb * MOE_BLOCK
    buf_tok = jnp.zeros((P,), jnp.int32).at[dest].set(flat_tok[order])
    buf_w = jnp.zeros((P,), h.dtype).at[dest].set(flat_w[order])
    block_e = jnp.minimum(
        jnp.searchsorted(pad_end, jnp.arange(nb, dtype=pad_end.dtype) * MOE_BLOCK, side='right'),
        N_EXPERTS - 1)
    y0 = (jax.nn.silu(xf @ ws_gate) * (xf @ ws_up)) @ ws_down

    def step(y, blk):
        tok, e, w = blk
        xb = xf[tok]
        hb = jax.nn.silu(xb @ w_gate[e]) * (xb @ w_up[e])
        return y.at[tok].add((hb @ w_down[e]) * w[:, None]), None

    y, _ = lax.scan(step, y0, (buf_tok.reshape(nb, MOE_BLOCK), block_e,
                               buf_w.reshape(nb, MOE_BLOCK)))
    return y.reshape(B, S, D)


def setup_inputs(seed: int = 0) -> dict:
    key = jax.random.key(seed)
    ks = iter(jax.random.split(key, 32))
    nrm = lambda shape, s: jax.random.normal(next(ks), shape, jnp.float32) * s
    D, E, F = D_MODEL, N_EXPERTS, EXPERT_DIM
    NA, NS = N_DA_LAYERS, N_SG_LAYERS
    return {
        "x": nrm((BATCH, SEQ, D), 1.0),
        "c": nrm((BATCH, D), 1.0),
        "positions": jnp.broadcast_to(jnp.arange(SEQ, dtype=jnp.int32), (BATCH, SEQ)),
        "norm1_g": 1.0 + nrm((DEPTH, D), 0.02),
        "norm2_g": 1.0 + nrm((DEPTH, D), 0.02),
        "ada_w": nrm((DEPTH, D, 6 * D), 0.5 * D ** -0.5),
        "ada_b": nrm((DEPTH, 6 * D), 0.02),
        "da_w_in": nrm((NA, D, 2 * DA_QK_WIDTH + DA_V_WIDTH), D ** -0.5),
        "da_lam_q1": nrm((NA, DA_HEAD_DIM), 0.1),
        "da_lam_k1": nrm((NA, DA_HEAD_DIM), 0.1),
        "da_lam_q2": nrm((NA, DA_HEAD_DIM), 0.1),
        "da_lam_k2": nrm((NA, DA_HEAD_DIM), 0.1),
        "da_subln_g": 1.0 + nrm((NA, 2 * DA_HEAD_DIM), 0.02),
        "da_w_out": nrm((NA, DA_V_WIDTH, D), DA_V_WIDTH ** -0.5),
        "sg_w_in": nrm((NS, D, 2 * SG_HALF), D ** -0.5),
        "sg_ln_g": 1.0 + nrm((NS, SG_HALF), 0.02),
        "sg_ln_b": nrm((NS, SG_HALF), 0.02),
        "sg_w_s": nrm((NS, SG_GROUPS, SG_CHUNK, SG_CHUNK), SG_CHUNK ** -0.5),
        "sg_b_s": nrm((NS, SG_GROUPS, SG_CHUNK), 0.02),
        "sg_w_out": nrm((NS, SG_HALF, D), SG_HALF ** -0.5),
        "moe_w_router": nrm((DEPTH, D, E), D ** -0.5),
        "moe_router_bias": nrm((DEPTH, E), 0.01),
        "moe_w_gate": nrm((DEPTH, E, D, F), D ** -0.5),
        "moe_w_up": nrm((DEPTH, E, D, F), D ** -0.5),
        "moe_w_down": nrm((DEPTH, E, F, D), F ** -0.5),
        "moe_ws_gate": nrm((DEPTH, D, SHARED_DIM), D ** -0.5),
        "moe_ws_up": nrm((DEPTH, D, SHARED_DIM), D ** -0.5),
        "moe_ws_down": nrm((DEPTH, SHARED_DIM, D), SHARED_DIM ** -0.5),
        "final_g": 1.0 + nrm((D,), 0.02),
    }


def reference(x, c, positions, norm1_g, norm2_g, ada_w, ada_b,
              da_w_in, da_lam_q1, da_lam_k1, da_lam_q2, da_lam_k2, da_subln_g, da_w_out,
              sg_w_in, sg_ln_g, sg_ln_b, sg_w_s, sg_b_s, sg_w_out,
              moe_w_router, moe_router_bias, moe_w_gate, moe_w_up, moe_w_down,
              moe_ws_gate, moe_ws_up, moe_ws_down, final_g):
    cond = jax.nn.silu(c)
    for i in range(DEPTH):
        mod = cond @ ada_w[i] + ada_b[i]
        sh1, sc1, g1, sh2, sc2, g2 = jnp.split(mod, 6, axis=-1)
        h = modulate(rms_norm(x, norm1_g[i]), sh1, sc1)
        j = i // N_MIXERS
        if i % N_MIXERS == 0:
            lambda_init = 0.8 - 0.6 * math.exp(-0.3 * i)
            mix = diff_attention(h, positions, da_w_in[j], da_lam_q1[j], da_lam_k1[j],
                                 da_lam_q2[j], da_lam_k2[j], da_subln_g[j], da_w_out[j],
                                 lambda_init)
        else:
            mix = chunked_spatial_gating(h, sg_w_in[j], sg_ln_g[j], sg_ln_b[j],
                                         sg_w_s[j], sg_b_s[j], sg_w_out[j])
        x = x + g1[:, None, :] * mix
        h = modulate(rms_norm(x, norm2_g[i]), sh2, sc2)
        x = x + g2[:, None, :] * moe(h, moe_w_router[i], moe_router_bias[i], moe_w_gate[i],
                                     moe_w_up[i], moe_w_down[i], moe_ws_gate[i],
                                     moe_ws_up[i], moe_ws_down[i])
    return rms_norm(x, final_g)
```

```python
import functools
import math

import jax
import jax.numpy as jnp
from jax import lax
from jax.experimental import pallas as pl
from jax.experimental.pallas import tpu as pltpu

F32 = jnp.float32
BF16 = jnp.bfloat16
HIGHEST = lax.Precision.HIGHEST

EPS = 1e-6
LANES = 128
N_EXPERTS = 64
TOP_K = 6
N_EXPERT_GROUPS = 8
TOPK_GROUPS = 4
GROUP_SIZE = N_EXPERTS // N_EXPERT_GROUPS
ROUTED_SCALE = 2.5
DA_HEAD_DIM = 64
ROPE_THETA = 10000.0
SG_CHUNK = 128
SG_GROUPS = 8
N_MOD = 8

MOE_ROWS = 256
VMEM_LIMIT = 56 * 1024 * 1024


def _params(*sem):
    return pltpu.CompilerParams(dimension_semantics=sem, vmem_limit_bytes=VMEM_LIMIT)


def _const_spec(shape):
    n = len(shape)
    return pl.BlockSpec(shape, lambda *_: (0,) * n, pipeline_mode=pl.Buffered(1))


def _token_tile(s, target):
    t = min(s, target)
    assert s % t == 0
    return t


def _norm_mod(x, g, shift, scale):
    y = x * lax.rsqrt(jnp.mean(x * x, axis=-1, keepdims=True) + EPS) * g
    return y * (1.0 + scale) + shift


def _mod_kernel(c_ref, w_ref, b_ref, o_ref):
    c = c_ref[...]
    cond = c * jax.nn.sigmoid(c)
    o_ref[0] = jnp.dot(cond, w_ref[0], preferred_element_type=F32, precision=HIGHEST) + b_ref[0]


def _ada_mod(c, ada_w, ada_b):
    depth, d, d6 = ada_w.shape
    b = c.shape[0]
    nj = d6 // d
    out = pl.pallas_call(
        _mod_kernel,
        grid=(depth, nj),
        in_specs=[
            pl.BlockSpec((b, d), lambda i, j: (0, 0)),
            pl.BlockSpec((1, d, d), lambda i, j: (i, 0, j)),
            pl.BlockSpec((1, 1, d), lambda i, j: (i, 0, j)),
        ],
        out_specs=pl.BlockSpec((1, b, d), lambda i, j: (i, 0, j)),
        out_shape=jax.ShapeDtypeStruct((depth, b, d6), F32),
        compiler_params=_params("arbitrary", "arbitrary"),
        name="ada_mod",
    )(c, ada_w, ada_b.reshape(depth, 1, d6))
    mod = out.reshape(depth, b, nj, d)
    return jnp.pad(mod, ((0, 0), (0, 0), (0, N_MOD - nj), (0, 0)))


def _qkv_kernel(x_ref, mod_ref, g_ref, pos_ref, freq_ref, w_ref, q_ref, k_ref, v_ref):
    x = x_ref[0]
    mod = mod_ref[0]
    d = x.shape[-1]
    h = _norm_mod(x, g_ref[...], mod[0:1], mod[1:2])
    qkv = jnp.dot(h.astype(BF16), w_ref[...], preferred_element_type=F32)
    ang = pos_ref[0].astype(F32) * freq_ref[...]
    cos = jnp.cos(ang)
    sin = jnp.sin(ang)
    lane = lax.broadcasted_iota(jnp.int32, ang.shape, 1)
    first_half = (lane % DA_HEAD_DIM) < (DA_HEAD_DIM // 2)
    sin_signed = jnp.where(first_half, -sin, sin)
    half = DA_HEAD_DIM // 2

    def rope(blk):
        partner = jnp.where(first_half, pltpu.roll(blk, LANES - half, 1), pltpu.roll(blk, half, 1))
        return blk * cos + partner * sin_signed

    q_scale = DA_HEAD_DIM ** -0.5
    for cb in range(d // LANES):
        lo = cb * LANES
        q_ref[0, :, lo:lo + LANES] = (rope(qkv[:, lo:lo + LANES]) * q_scale).astype(BF16)
        k_ref[0, :, lo:lo + LANES] = rope(qkv[:, d + lo:d + lo + LANES]).astype(BF16)
    v_ref[0] = qkv[:, 2 * d:].astype(BF16)


def _qkv_rope(x, modv, norm_g, positions, w_in_bf16):
    b, s, d = x.shape
    tm = _token_tile(s, 512)
    inv_freq = ROPE_THETA ** (-jnp.arange(0, DA_HEAD_DIM, 2, dtype=F32) / DA_HEAD_DIM)
    freq = jnp.tile(inv_freq, LANES // (DA_HEAD_DIM // 2)).reshape(1, LANES)
    tok = pl.BlockSpec((1, tm, d), lambda bi, si: (bi, si, 0))
    out = jax.ShapeDtypeStruct((b, s, d), BF16)
    return pl.pallas_call(
        _qkv_kernel,
        grid=(b, s // tm),
        in_specs=[
            tok,
            pl.BlockSpec((1, N_MOD, d), lambda bi, si: (bi, 0, 0)),
            _const_spec((1, d)),
            pl.BlockSpec((1, tm, 1), lambda bi, si: (bi, si, 0)),
            _const_spec((1, LANES)),
            _const_spec((d, 3 * d)),
        ],
        out_specs=[tok, tok, tok],
        out_shape=[out, out, out],
        compiler_params=_params("parallel", "parallel"),
        name="qkv_rope",
    )(x, modv, norm_g.reshape(1, d), positions.reshape(b, s, 1), freq, w_in_bf16)


def _attn_kernel(lam_ref, q_ref, k_ref, v_ref, g_ref, o_ref, qs_ref, m_ref, l_ref, acc_ref,
                 *, lambda_init, tq, tk):
    s_len = k_ref.shape[1]
    q = q_ref[0]
    lane = lax.broadcasted_iota(jnp.int32, q.shape, 1)
    zero = jnp.zeros_like(q)
    qs_ref[0:tq] = jnp.where(lane < DA_HEAD_DIM, q, zero)
    qs_ref[tq:2 * tq] = jnp.where(lane >= DA_HEAD_DIM, q, zero)
    m_ref[...] = jnp.full(m_ref.shape, -jnp.inf, F32)
    l_ref[...] = jnp.zeros(l_ref.shape, F32)
    acc_ref[...] = jnp.zeros(acc_ref.shape, F32)

    def step(j, carry):
        off = pl.multiple_of(j * tk, tk)
        kb = k_ref[0, pl.ds(off, tk), :]
        vb = v_ref[0, pl.ds(off, tk), :]
        s = lax.dot_general(qs_ref[...], kb, (((1,), (1,)), ((), ())),
                            preferred_element_type=F32)
        m_prev = m_ref[...]
        m_new = jnp.maximum(m_prev, jnp.max(s, axis=-1, keepdims=True))
        alpha = jnp.exp(m_prev - m_new)
        p = jnp.exp(s - m_new)
        l_ref[...] = alpha * l_ref[...] + jnp.sum(p, axis=-1, keepdims=True)
        acc_ref[...] = alpha * acc_ref[...] + jnp.dot(p.astype(BF16), vb, preferred_element_type=F32)
        m_ref[...] = m_new
        return carry

    lax.fori_loop(0, s_len // tk, step, 0)

    o = acc_ref[...] / l_ref[...]
    lp = lam_ref[...]
    lam = (jnp.exp(jnp.sum(lp[0:1] * lp[1:2], axis=-1, keepdims=True))
           - jnp.exp(jnp.sum(lp[2:3] * lp[3:4], axis=-1, keepdims=True)) + lambda_init)
    diff = o[0:tq] - lam * o[tq:2 * tq]
    y = diff * lax.rsqrt(jnp.mean(diff * diff, axis=-1, keepdims=True) + EPS) * g_ref[...]
    o_ref[0] = (y * (1.0 - lambda_init)).astype(BF16)


def _diff_attention(q, k, v, lam_params, subln_g, lambda_init):
    b, s, d = q.shape
    hw = 2 * DA_HEAD_DIM
    nh = d // hw
    tq = _token_tile(s, 512)
    tk = _token_tile(s, 512)
    kern = functools.partial(_attn_kernel, lambda_init=lambda_init, tq=tq, tk=tk)
    return pl.pallas_call(
        kern,
        grid=(b, nh, s // tq),
        in_specs=[
            _const_spec((4, DA_HEAD_DIM)),
            pl.BlockSpec((1, tq, hw), lambda bi, hi, qi: (bi, qi, hi)),
            pl.BlockSpec((1, s, hw), lambda bi, hi, qi: (bi, 0, hi)),
            pl.BlockSpec((1, s, hw), lambda bi, hi, qi: (bi, 0, hi)),
            _const_spec((1, hw)),
        ],
        out_specs=pl.BlockSpec((1, tq, hw), lambda bi, hi, qi: (bi, qi, hi)),
        out_shape=jax.ShapeDtypeStruct((b, s, d), BF16),
        scratch_shapes=[
            pltpu.VMEM((2 * tq, hw), BF16),
            pltpu.VMEM((2 * tq, 1), F32),
            pltpu.VMEM((2 * tq, 1), F32),
            pltpu.VMEM((2 * tq, hw), F32),
        ],
        compiler_params=_params("parallel", "parallel", "arbitrary"),
        name="diff_attn",
    )(lam_params, q, k, v, subln_g.reshape(1, hw))


def _outproj_kernel(o_ref, x_ref, mod_ref, w_ref, x1_ref):
    mix = jnp.dot(o_ref[0], w_ref[...], preferred_element_type=F32)
    x1_ref[0] = x_ref[0] + mod_ref[0][2:3] * mix


def _outproj_residual(o, x, modv, w_out_bf16):
    b, s, d = x.shape
    tm = _token_tile(s, 512)
    tok = pl.BlockSpec((1, tm, d), lambda bi, si: (bi, si, 0))
    return pl.pallas_call(
        _outproj_kernel,
        grid=(b, s // tm),
        in_specs=[tok, tok, pl.BlockSpec((1, N_MOD, d), lambda bi, si: (bi, 0, 0)), _const_spec((d, d))],
        out_specs=tok,
        out_shape=jax.ShapeDtypeStruct((b, s, d), F32),
        compiler_params=_params("parallel", "parallel"),
        name="attn_outproj",
    )(o, x, modv, w_out_bf16)


def _sg_kernel(x_ref, mod_ref, g_ref, win_ref, lng_ref, lnb_ref, ws_ref, bs_ref, wout_ref,
               x1_ref, gated_ref):
    x = x_ref[0]
    mod = mod_ref[0]
    tm = x.shape[0]
    half = lng_ref.shape[-1]
    gdim = half // SG_GROUPS
    h = _norm_mod(x, g_ref[...], mod[0:1], mod[1:2])
    z = jnp.dot(h.astype(BF16), win_ref[...], preferred_element_type=F32)
    z = 0.5 * z * (1.0 + lax.erf(z * (2.0 ** -0.5)))
    u = z[:, :half]
    v = z[:, half:]
    mu = jnp.mean(v, axis=-1, keepdims=True)
    vc = v - mu
    v = vc * lax.rsqrt(jnp.mean(vc * vc, axis=-1, keepdims=True) + EPS) * lng_ref[...] + lnb_ref[...]
    vb = v.astype(BF16)
    for c in range(tm // SG_CHUNK):
        r0 = c * SG_CHUNK
        for g in range(SG_GROUPS):
            c0 = g * gdim
            sp = jnp.dot(ws_ref[g], vb[r0:r0 + SG_CHUNK, c0:c0 + gdim],
                         preferred_element_type=F32) + bs_ref[g]
            gated_ref[r0:r0 + SG_CHUNK, c0:c0 + gdim] = (
                u[r0:r0 + SG_CHUNK, c0:c0 + gdim] * sp).astype(BF16)
    mix = jnp.dot(gated_ref[...], wout_ref[...], preferred_element_type=F32)
    x1_ref[0] = x + mod[2:3] * mix


def _spatial_gating(x, modv, norm_g, w_in, ln_g, ln_b, w_s, b_s, w_out):
    b, s, d = x.shape
    half = ln_g.shape[-1]
    tm = _token_tile(s, 256)
    assert tm % SG_CHUNK == 0
    tok = pl.BlockSpec((1, tm, d), lambda bi, si: (bi, si, 0))
    return pl.pallas_call(
        _sg_kernel,
        grid=(b, s // tm),
        in_specs=[
            tok,
            pl.BlockSpec((1, N_MOD, d), lambda bi, si: (bi, 0, 0)),
            _const_spec((1, d)),
            _const_spec((d, 2 * half)),
            _const_spec((1, half)),
            _const_spec((1, half)),
            _const_spec((SG_GROUPS, SG_CHUNK, SG_CHUNK)),
            _const_spec((SG_GROUPS, SG_CHUNK, 1)),
            _const_spec((half, d)),
        ],
        out_specs=tok,
        out_shape=jax.ShapeDtypeStruct((b, s, d), F32),
        scratch_shapes=[pltpu.VMEM((tm, half), BF16)],
        compiler_params=_params("parallel", "parallel"),
        name="spatial_gating",
    )(x, modv, norm_g.reshape(1, d), w_in.astype(BF16), ln_g.reshape(1, half), ln_b.reshape(1, half),
      w_s.astype(BF16), b_s.reshape(SG_GROUPS, SG_CHUNK, 1), w_out.astype(BF16))


def _first_argmax(cur, idx, sentinel):
    m = jnp.max(cur, axis=0, keepdims=True)
    first = jnp.min(jnp.where(cur == m, idx, sentinel), axis=0, keepdims=True)
    return m, idx == first


def _router_kernel(x_ref, mod_ref, g_ref, wrt_ref, bias_ref, h_ref, wd_ref, mk_ref):
    mod = mod_ref[0]
    h = _norm_mod(x_ref[0], g_ref[...], mod[3:4], mod[4:5])
    h_ref[0] = h
    logits = lax.dot_general(wrt_ref[...], h, (((1,), (1,)), ((), ())),
                             precision=HIGHEST, preferred_element_type=F32)
    scores = jax.nn.sigmoid(logits)
    sel = scores + bias_ref[...]
    tm = sel.shape[1]
    neg = -jnp.inf
    sub = lax.broadcasted_iota(jnp.int32, (GROUP_SIZE, tm), 0)
    rows = []
    for g in range(N_EXPERT_GROUPS):
        blk = sel[g * GROUP_SIZE:(g + 1) * GROUP_SIZE]
        m1, hit = _first_argmax(blk, sub, GROUP_SIZE)
        m2 = jnp.max(jnp.where(hit, neg, blk), axis=0, keepdims=True)
        rows.append(m1 + m2)
    cur = jnp.concatenate(rows, axis=0)
    gsel = jnp.zeros(cur.shape, jnp.int32)
    for _ in range(TOPK_GROUPS):
        _, hit = _first_argmax(cur, sub, N_EXPERT_GROUPS)
        gsel = jnp.where(hit, 1, gsel)
        cur = jnp.where(hit, neg, cur)
    masked = []
    for g in range(N_EXPERT_GROUPS):
        blk = sel[g * GROUP_SIZE:(g + 1) * GROUP_SIZE]
        masked.append(jnp.where(gsel[g:g + 1] > 0, blk, neg))
    cur = jnp.concatenate(masked, axis=0)
    eidx = lax.broadcasted_iota(jnp.int32, cur.shape, 0)
    chosen = jnp.zeros(cur.shape, jnp.int32)
    for _ in range(TOP_K):
        _, hit = _first_argmax(cur, eidx, N_EXPERTS)
        chosen = jnp.where(hit, 1, chosen)
        cur = jnp.where(hit, neg, cur)
    picked = jnp.where(chosen > 0, scores, 0.0)
    wsum = jnp.sum(picked, axis=0, keepdims=True)
    wd_ref[...] = picked / wsum * ROUTED_SCALE
    mk_ref[...] = chosen


def _router(x, modv, norm_g, w_router, router_bias):
    b, s, d = x.shape
    t = b * s
    e = w_router.shape[1]
    tm = _token_tile(s, 512)
    nst = s // tm
    tok = pl.BlockSpec((1, tm, d), lambda bi, si: (bi, si, 0))
    et = pl.BlockSpec((e, tm), lambda bi, si: (0, bi * nst + si))
    return pl.pallas_call(
        _router_kernel,
        grid=(b, nst),
        in_specs=[
            tok,
            pl.BlockSpec((1, N_MOD, d), lambda bi, si: (bi, 0, 0)),
            _const_spec((1, d)),
            _const_spec((e, d)),
            _const_spec((e, 1)),
        ],
        out_specs=[tok, et, et],
        out_shape=[jax.ShapeDtypeStruct((b, s, d), F32),
                   jax.ShapeDtypeStruct((e, t), F32),
                   jax.ShapeDtypeStruct((e, t), jnp.int32)],
        compiler_params=_params("parallel", "parallel"),
        name="moe_router",
    )(x, modv, norm_g.reshape(1, d), w_router.T, router_bias.reshape(e, 1))


def _dispatch_plan(mask, wdense, n_blocks):
    e, t = mask.shape
    cnt = jnp.cumsum(mask, axis=1)
    counts = cnt[:, -1]
    padded = (counts + MOE_ROWS - 1) // MOE_ROWS * MOE_ROWS
    pad_end = jnp.cumsum(padded)
    pad_start = pad_end - padded
    dest_dense = pad_start[:, None] + cnt - mask
    ordinal = jnp.cumsum(mask, axis=0) - mask
    dest, wts = [], []
    for k in range(TOP_K):
        pick = (mask > 0) & (ordinal == k)
        dest.append(jnp.sum(jnp.where(pick, dest_dense, 0), axis=0))
        wts.append(jnp.sum(jnp.where(pick, wdense, 0.0), axis=0))
    dest = jnp.stack(dest, axis=1).astype(jnp.int32)
    wts = jnp.stack(wts, axis=1)
    blk_start = jnp.arange(n_blocks, dtype=jnp.int32) * MOE_ROWS
    block_e = jnp.minimum(jnp.searchsorted(pad_end, blk_start, side="right"), e - 1).astype(jnp.int32)
    slack = padded - counts
    slack_end = jnp.cumsum(slack)
    q = jnp.arange(n_blocks * MOE_ROWS - t * TOP_K, dtype=jnp.int32)
    eq = jnp.searchsorted(slack_end, q, side="right")
    ec = jnp.minimum(eq, e - 1)
    in_expert = pad_start[ec] + counts[ec] + q - (slack_end[ec] - slack[ec])
    pad_rows = jnp.where(eq < e, in_expert, pad_end[-1] + q - slack_end[-1]).astype(jnp.int32)
    return dest, wts, block_e, pad_rows


def _row_copy(src, dst, s_row, d_row, sem):
    return pltpu.make_async_copy(src.at[pl.ds(s_row, 1)], dst.at[pl.ds(d_row, 1)], sem)


def _drain(src, dst, sem, rows, repeats):
    for _ in range(repeats):
        pltpu.make_async_copy(src.at[pl.ds(0, rows)], dst.at[pl.ds(0, rows)], sem).wait()


def _dispatch_kernel(dest_ref, pads_ref, src_ref, zero_ref, dst_ref, sem, *, tokens, pads):
    base = pl.program_id(0) * tokens

    def body(i, carry):
        for k in range(TOP_K):
            _row_copy(src_ref, dst_ref, base + i, dest_ref[i * TOP_K + k], sem).start()
        return carry

    lax.fori_loop(0, tokens, body, 0)

    def fill(i, carry):
        _row_copy(zero_ref, dst_ref, 0, pads_ref[i], sem).start()
        return carry

    lax.fori_loop(0, pads, fill, 0)
    _drain(src_ref, dst_ref, sem, tokens, TOP_K)
    _drain(dst_ref, dst_ref, sem, pads, 1)


def _dispatch(dest_flat, pad_rows, src, out_rows, tokens):
    t, d = src.shape
    steps = t // tokens
    pads = pad_rows.shape[0] // steps
    assert pads * steps == pad_rows.shape[0]
    return pl.pallas_call(
        functools.partial(_dispatch_kernel, tokens=tokens, pads=pads),
        grid=(steps,),
        in_specs=[
            pl.BlockSpec((tokens * TOP_K,), lambda i: (i,), memory_space=pltpu.SMEM),
            pl.BlockSpec((pads,), lambda i: (i,), memory_space=pltpu.SMEM),
            pl.BlockSpec(memory_space=pl.ANY),
            pl.BlockSpec(memory_space=pl.ANY),
        ],
        out_specs=pl.BlockSpec(memory_space=pl.ANY),
        out_shape=jax.ShapeDtypeStruct((out_rows, d), src.dtype),
        scratch_shapes=[pltpu.SemaphoreType.DMA(())],
        compiler_params=_params("arbitrary"),
        name="moe_dispatch",
    )(dest_flat, pad_rows, src, jnp.zeros((8, d), src.dtype))


def _collect_kernel(dest_ref, src_ref, dst_ref, sem, *, tokens, total):
    base = pl.program_id(0) * tokens

    def body(i, carry):
        for k in range(TOP_K):
            _row_copy(src_ref, dst_ref, dest_ref[i * TOP_K + k], k * total + base + i, sem).start()
        return carry

    lax.fori_loop(0, tokens, body, 0)
    _drain(src_ref, dst_ref, sem, tokens, TOP_K)


def _collect(dest_flat, src, tokens):
    t = dest_flat.shape[0] // TOP_K
    d = src.shape[1]
    return pl.pallas_call(
        functools.partial(_collect_kernel, tokens=tokens, total=t),
        grid=(t // tokens,),
        in_specs=[
            pl.BlockSpec((tokens * TOP_K,), lambda i: (i,), memory_space=pltpu.SMEM),
            pl.BlockSpec(memory_space=pl.ANY),
        ],
        out_specs=pl.BlockSpec(memory_space=pl.ANY),
        out_shape=jax.ShapeDtypeStruct((TOP_K * t, d), src.dtype),
        scratch_shapes=[pltpu.SemaphoreType.DMA(())],
        compiler_params=_params("arbitrary"),
        name="moe_collect",
    )(dest_flat, src)


def _expert_kernel(be_ref, xs_ref, wg_ref, wu_ref, wd_ref, o_ref):
    xb = xs_ref[...].astype(BF16)
    gate = jnp.dot(xb, wg_ref[0], preferred_element_type=F32)
    up = jnp.dot(xb, wu_ref[0], preferred_element_type=F32)
    hb = (gate * jax.nn.sigmoid(gate) * up).astype(BF16)
    o_ref[...] = jnp.dot(hb, wd_ref[0], preferred_element_type=F32)


def _experts(xs, block_e, w_gate, w_up, w_down):
    p, d = xs.shape
    f = w_gate.shape[-1]
    nb = p // MOE_ROWS
    rows = pl.BlockSpec((MOE_ROWS, d), lambda i, be: (i, 0))
    return pl.pallas_call(
        _expert_kernel,
        grid_spec=pltpu.PrefetchScalarGridSpec(
            num_scalar_prefetch=1,
            grid=(nb,),
            in_specs=[
                rows,
                pl.BlockSpec((1, d, f), lambda i, be: (be[i], 0, 0)),
                pl.BlockSpec((1, d, f), lambda i, be: (be[i], 0, 0)),
                pl.BlockSpec((1, f, d), lambda i, be: (be[i], 0, 0)),
            ],
            out_specs=rows,
        ),
        out_shape=jax.ShapeDtypeStruct((p, d), F32),
        compiler_params=_params("arbitrary"),
        name="moe_experts",
    )(block_e, xs, w_gate, w_up, w_down)


def _combine_kernel(x_ref, h_ref, eo_ref, w_ref, mod_ref, sg_ref, su_ref, sd_ref, fg_ref, o_ref,
                    *, final_norm):
    hb = h_ref[0].astype(BF16)
    gate = jnp.dot(hb, sg_ref[...], preferred_element_type=F32)
    up = jnp.dot(hb, su_ref[...], preferred_element_type=F32)
    y = jnp.dot((gate * jax.nn.sigmoid(gate) * up).astype(BF16), sd_ref[...], preferred_element_type=F32)
    w = w_ref[...]
    for k in range(TOP_K):
        y = y + eo_ref[k] * w[:, k:k + 1]
    out = x_ref[0] + mod_ref[0][5:6] * y
    if final_norm:
        out = out * lax.rsqrt(jnp.mean(out * out, axis=-1, keepdims=True) + EPS) * fg_ref[...]
    o_ref[0] = out


def _combine(x, h, eo, wts, modv, ws_gate, ws_up, ws_down, final_g, final_norm):
    b, s, d = x.shape
    f = ws_gate.shape[-1]
    tm = _token_tile(s, 512)
    nst = s // tm
    tok = pl.BlockSpec((1, tm, d), lambda bi, si: (bi, si, 0))
    return pl.pallas_call(
        functools.partial(_combine_kernel, final_norm=final_norm),
        grid=(b, nst),
        in_specs=[
            tok, tok,
            pl.BlockSpec((TOP_K, tm, d), lambda bi, si: (0, bi * nst + si, 0)),
            pl.BlockSpec((tm, TOP_K), lambda bi, si: (bi * nst + si, 0)),
            pl.BlockSpec((1, N_MOD, d), lambda bi, si: (bi, 0, 0)),
            _const_spec((d, f)), _const_spec((d, f)), _const_spec((f, d)),
            _const_spec((1, d)),
        ],
        out_specs=tok,
        out_shape=jax.ShapeDtypeStruct((b, s, d), F32),
        compiler_params=_params("parallel", "parallel"),
        name="moe_combine",
    )(x, h, eo, wts, modv, ws_gate.astype(BF16), ws_up.astype(BF16), ws_down.astype(BF16),
      final_g.reshape(1, d))


def _moe_layer(x, modv, norm_g, w_router, router_bias, w_gate, w_up, w_down,
               ws_gate, ws_up, ws_down, final_g, final_norm):
    b, s, d = x.shape
    t = b * s
    nb = pl.cdiv(t * TOP_K, MOE_ROWS) + N_EXPERTS
    h, wdense, mask = _router(x, modv, norm_g, w_router, router_bias)
    dest, wts, block_e, pad_rows = _dispatch_plan(mask, wdense, nb)
    dest_flat = dest.reshape(t * TOP_K)
    tokens = _token_tile(t, 4096)
    xs = _dispatch(dest_flat, pad_rows, h.reshape(t, d), nb * MOE_ROWS, tokens)
    eo = _experts(xs, block_e, w_gate.astype(BF16), w_up.astype(BF16), w_down.astype(BF16))
    eo_tok = _collect(dest_flat, eo, tokens)
    return _combine(x, h, eo_tok.reshape(TOP_K, t, d), wts, modv, ws_gate, ws_up, ws_down,
                    final_g, final_norm)


def kernel(x, c, positions, norm1_g, norm2_g, ada_w, ada_b, da_w_in, da_lam_q1, da_lam_k1, da_lam_q2, da_lam_k2, da_subln_g, da_w_out, sg_w_in, sg_ln_g, sg_ln_b, sg_w_s, sg_b_s, sg_w_out, moe_w_router, moe_router_bias, moe_w_gate, moe_w_up, moe_w_down, moe_ws_gate, moe_ws_up, moe_ws_down, final_g):
    depth = ada_w.shape[0]
    mod = _ada_mod(c, ada_w, ada_b)
    for i in range(depth):
        j = i // 2
        if i % 2 == 0:
            lambda_init = 0.8 - 0.6 * math.exp(-0.3 * i)
            q, k, v = _qkv_rope(x, mod[i], norm1_g[i], positions, da_w_in[j].astype(BF16))
            lam_params = jnp.stack([da_lam_q1[j], da_lam_k1[j], da_lam_q2[j], da_lam_k2[j]])
            o = _diff_attention(q, k, v, lam_params, da_subln_g[j], lambda_init)
            x = _outproj_residual(o, x, mod[i], da_w_out[j].astype(BF16))
        else:
            x = _spatial_gating(x, mod[i], norm1_g[i], sg_w_in[j], sg_ln_g[j], sg_ln_b[j],
                                sg_w_s[j], sg_b_s[j], sg_w_out[j])
        x = _moe_layer(x, mod[i], norm2_g[i], moe_w_router[i], moe_router_bias[i],
                       moe_w_gate[i], moe_w_up[i], moe_w_down[i],
                       moe_ws_gate[i], moe_ws_up[i], moe_ws_down[i],
                       final_g, final_norm=(i == depth - 1))
    return x
```

```python
import functools
import math

import jax
import jax.numpy as jnp
from jax import lax
from jax.experimental import pallas as pl
from jax.experimental.pallas import tpu as pltpu

F32 = jnp.float32
BF16 = jnp.bfloat16
HIGHEST = lax.Precision.HIGHEST

EPS = 1e-6
LANES = 128
N_EXPERTS = 64
TOP_K = 6
N_EXPERT_GROUPS = 8
TOPK_GROUPS = 4
GROUP_SIZE = N_EXPERTS // N_EXPERT_GROUPS
ROUTED_SCALE = 2.5
DA_HEAD_DIM = 64
ROPE_THETA = 10000.0
SG_CHUNK = 128
SG_GROUPS = 8
N_MOD = 8

MOE_ROWS = 256
VMEM_LIMIT = 56 * 1024 * 1024


def _params(*sem):
    return pltpu.CompilerParams(dimension_semantics=sem, vmem_limit_bytes=VMEM_LIMIT)


def _const_spec(shape):
    n = len(shape)
    return pl.BlockSpec(shape, lambda *_: (0,) * n, pipeline_mode=pl.Buffered(1))


def _token_tile(s, target):
    t = min(s, target)
    assert s % t == 0
    return t


def _rows_to_tiles(ref, val):
    n, d = val.shape
    per = d // LANES
    for i in range(per):
        ref[pl.ds(i, n, stride=per), :] = val[:, i * LANES:(i + 1) * LANES]


def _rows_from_tiles(ref, n, per, lead=()):
    return jnp.concatenate([ref[lead + (pl.ds(i, n, stride=per), slice(None))] for i in range(per)],
                           axis=-1)


def _norm_mod(x, g, shift, scale):
    y = x * lax.rsqrt(jnp.mean(x * x, axis=-1, keepdims=True) + EPS) * g
    return y * (1.0 + scale) + shift


def _mod_kernel(c_ref, w_ref, b_ref, o_ref):
    c = c_ref[...]
    cond = c * jax.nn.sigmoid(c)
    o_ref[0] = jnp.dot(cond, w_ref[0], preferred_element_type=F32, precision=HIGHEST) + b_ref[0]


def _ada_mod(c, ada_w, ada_b):
    depth, d, d6 = ada_w.shape
    b = c.shape[0]
    nj = d6 // d
    out = pl.pallas_call(
        _mod_kernel,
        grid=(depth, nj),
        in_specs=[
            pl.BlockSpec((b, d), lambda i, j: (0, 0)),
            pl.BlockSpec((1, d, d), lambda i, j: (i, 0, j)),
            pl.BlockSpec((1, 1, d), lambda i, j: (i, 0, j)),
        ],
        out_specs=pl.BlockSpec((1, b, d), lambda i, j: (i, 0, j)),
        out_shape=jax.ShapeDtypeStruct((depth, b, d6), F32),
        compiler_params=_params("arbitrary", "arbitrary"),
        name="ada_mod",
    )(c, ada_w, ada_b.reshape(depth, 1, d6))
    mod = out.reshape(depth, b, nj, d)
    return jnp.pad(mod, ((0, 0), (0, 0), (0, N_MOD - nj), (0, 0)))


def _qkv_kernel(x_ref, mod_ref, g_ref, pos_ref, freq_ref, w_ref, q_ref, k_ref, v_ref):
    x = x_ref[0]
    mod = mod_ref[0]
    d = x.shape[-1]
    h = _norm_mod(x, g_ref[...], mod[0:1], mod[1:2])
    qkv = jnp.dot(h.astype(BF16), w_ref[...], preferred_element_type=F32)
    ang = pos_ref[0].astype(F32) * freq_ref[...]
    cos = jnp.cos(ang)
    sin = jnp.sin(ang)
    lane = lax.broadcasted_iota(jnp.int32, ang.shape, 1)
    first_half = (lane % DA_HEAD_DIM) < (DA_HEAD_DIM // 2)
    sin_signed = jnp.where(first_half, -sin, sin)
    half = DA_HEAD_DIM // 2

    def rope(blk):
        partner = jnp.where(first_half, pltpu.roll(blk, LANES - half, 1), pltpu.roll(blk, half, 1))
        return blk * cos + partner * sin_signed

    q_scale = DA_HEAD_DIM ** -0.5 * math.log2(math.e)
    for cb in range(d // LANES):
        lo = cb * LANES
        q_ref[0, :, lo:lo + LANES] = (rope(qkv[:, lo:lo + LANES]) * q_scale).astype(BF16)
        k_ref[0, :, lo:lo + LANES] = rope(qkv[:, d + lo:d + lo + LANES]).astype(BF16)
    v_ref[0] = qkv[:, 2 * d:].astype(BF16)


def _qkv_rope(x, modv, norm_g, positions, w_in_bf16):
    b, s, d = x.shape
    tm = _token_tile(s, 512)
    inv_freq = ROPE_THETA ** (-jnp.arange(0, DA_HEAD_DIM, 2, dtype=F32) / DA_HEAD_DIM)
    freq = jnp.tile(inv_freq, LANES // (DA_HEAD_DIM // 2)).reshape(1, LANES)
    tok = pl.BlockSpec((1, tm, d), lambda bi, si: (bi, si, 0))
    out = jax.ShapeDtypeStruct((b, s, d), BF16)
    return pl.pallas_call(
        _qkv_kernel,
        grid=(b, s // tm),
        in_specs=[
            tok,
            pl.BlockSpec((1, N_MOD, d), lambda bi, si: (bi, 0, 0)),
            _const_spec((1, d)),
            pl.BlockSpec((1, tm, 1), lambda bi, si: (bi, si, 0)),
            _const_spec((1, LANES)),
            _const_spec((d, 3 * d)),
        ],
        out_specs=[tok, tok, tok],
        out_shape=[out, out, out],
        compiler_params=_params("parallel", "parallel"),
        name="qkv_rope",
    )(x, modv, norm_g.reshape(1, d), positions.reshape(b, s, 1), freq, w_in_bf16)


ATTN_ROW_CHUNK = 64


def _attn_kernel(lam_ref, q_ref, k_ref, v_ref, g_ref, o_ref, qs_ref, s_ref, p_ref, m_ref, l_ref, acc_ref,
                 *, lambda_init, tq, tk):
    s_len = k_ref.shape[1]
    rows = 2 * tq
    q = q_ref[0]
    lane = lax.broadcasted_iota(jnp.int32, q.shape, 1)
    zero = jnp.zeros_like(q)
    qs_ref[0:tq] = jnp.where(lane < DA_HEAD_DIM, q, zero)
    qs_ref[tq:rows] = jnp.where(lane >= DA_HEAD_DIM, q, zero)
    m_ref[...] = jnp.full(m_ref.shape, -jnp.inf, F32)
    l_ref[...] = jnp.zeros(l_ref.shape, F32)
    acc_ref[...] = jnp.zeros(acc_ref.shape, F32)

    def step(j, carry):
        off = pl.multiple_of(j * tk, tk)
        kb = k_ref[0, pl.ds(off, tk), :]
        vb = v_ref[0, pl.ds(off, tk), :]
        s_ref[...] = lax.dot_general(qs_ref[...], kb, (((1,), (1,)), ((), ())),
                                     preferred_element_type=F32)
        for r0 in range(0, rows, ATTN_ROW_CHUNK):
            rs = slice(r0, r0 + ATTN_ROW_CHUNK)
            mx = s_ref[rs, 0:LANES]
            for c0 in range(LANES, tk, LANES):
                mx = jnp.maximum(mx, s_ref[rs, c0:c0 + LANES])
            m_prev = m_ref[rs]
            m_new = jnp.maximum(m_prev, jnp.max(mx, axis=-1, keepdims=True))
            alpha = jnp.exp2(m_prev - m_new)
            psum = None
            for c0 in range(0, tk, LANES):
                p = jnp.exp2(s_ref[rs, c0:c0 + LANES] - m_new)
                psum = p if psum is None else psum + p
                p_ref[rs, c0:c0 + LANES] = p.astype(BF16)
            l_ref[rs] = alpha * l_ref[rs] + psum
            m_ref[rs] = m_new
            acc_ref[rs] = alpha * acc_ref[rs]
        acc_ref[...] += jnp.dot(p_ref[...], vb, preferred_element_type=F32)
        return carry

    lax.fori_loop(0, s_len // tk, step, 0)

    o = acc_ref[...] / jnp.sum(l_ref[...], axis=-1, keepdims=True)
    lp = lam_ref[...]
    lam = (jnp.exp(jnp.sum(lp[0:1] * lp[1:2], axis=-1, keepdims=True))
           - jnp.exp(jnp.sum(lp[2:3] * lp[3:4], axis=-1, keepdims=True)) + lambda_init)
    diff = o[0:tq] - lam * o[tq:2 * tq]
    y = diff * lax.rsqrt(jnp.mean(diff * diff, axis=-1, keepdims=True) + EPS) * g_ref[...]
    o_ref[0] = (y * (1.0 - lambda_init)).astype(BF16)


def _diff_attention(q, k, v, lam_params, subln_g, lambda_init):
    b, s, d = q.shape
    hw = 2 * DA_HEAD_DIM
    nh = d // hw
    tq = _token_tile(s, 256)
    tk = _token_tile(s, 1024)
    kern = functools.partial(_attn_kernel, lambda_init=lambda_init, tq=tq, tk=tk)
    return pl.pallas_call(
        kern,
        grid=(b, nh, s // tq),
        in_specs=[
            _const_spec((4, DA_HEAD_DIM)),
            pl.BlockSpec((1, tq, hw), lambda bi, hi, qi: (bi, qi, hi)),
            pl.BlockSpec((1, s, hw), lambda bi, hi, qi: (bi, 0, hi)),
            pl.BlockSpec((1, s, hw), lambda bi, hi, qi: (bi, 0, hi)),
            _const_spec((1, hw)),
        ],
        out_specs=pl.BlockSpec((1, tq, hw), lambda bi, hi, qi: (bi, qi, hi)),
        out_shape=jax.ShapeDtypeStruct((b, s, d), BF16),
        scratch_shapes=[
            pltpu.VMEM((2 * tq, hw), BF16),
            pltpu.VMEM((2 * tq, tk), F32),
            pltpu.VMEM((2 * tq, tk), BF16),
            pltpu.VMEM((2 * tq, LANES), F32),
            pltpu.VMEM((2 * tq, LANES), F32),
            pltpu.VMEM((2 * tq, hw), F32),
        ],
        compiler_params=_params("parallel", "parallel", "arbitrary"),
        name="diff_attn",
    )(lam_params, q, k, v, subln_g.reshape(1, hw))


def _outproj_kernel(o_ref, x_ref, mod_ref, w_ref, x1_ref):
    mix = jnp.dot(o_ref[0], w_ref[...], preferred_element_type=F32)
    x1_ref[0] = x_ref[0] + mod_ref[0][2:3] * mix


def _outproj_residual(o, x, modv, w_out_bf16):
    b, s, d = x.shape
    tm = _token_tile(s, 512)
    tok = pl.BlockSpec((1, tm, d), lambda bi, si: (bi, si, 0))
    return pl.pallas_call(
        _outproj_kernel,
        grid=(b, s // tm),
        in_specs=[tok, tok, pl.BlockSpec((1, N_MOD, d), lambda bi, si: (bi, 0, 0)), _const_spec((d, d))],
        out_specs=tok,
        out_shape=jax.ShapeDtypeStruct((b, s, d), F32),
        compiler_params=_params("parallel", "parallel"),
        name="attn_outproj",
    )(o, x, modv, w_out_bf16)


def _sg_kernel(x_ref, mod_ref, g_ref, win_ref, lng_ref, lnb_ref, ws_ref, bs_ref, wout_ref,
               x1_ref, gated_ref):
    x = x_ref[0]
    mod = mod_ref[0]
    tm = x.shape[0]
    half = lng_ref.shape[-1]
    gdim = half // SG_GROUPS
    h = _norm_mod(x, g_ref[...], mod[0:1], mod[1:2])
    z = jnp.dot(h.astype(BF16), win_ref[...], preferred_element_type=F32)
    z = 0.5 * z * (1.0 + lax.erf(z * (2.0 ** -0.5)))
    u = z[:, :half]
    v = z[:, half:]
    mu = jnp.mean(v, axis=-1, keepdims=True)
    vc = v - mu
    v = vc * lax.rsqrt(jnp.mean(vc * vc, axis=-1, keepdims=True) + EPS) * lng_ref[...] + lnb_ref[...]
    vb = v.astype(BF16)
    for c in range(tm // SG_CHUNK):
        r0 = c * SG_CHUNK
        for g in range(SG_GROUPS):
            c0 = g * gdim
            sp = jnp.dot(ws_ref[g], vb[r0:r0 + SG_CHUNK, c0:c0 + gdim],
                         preferred_element_type=F32) + bs_ref[g]
            gated_ref[r0:r0 + SG_CHUNK, c0:c0 + gdim] = (
                u[r0:r0 + SG_CHUNK, c0:c0 + gdim] * sp).astype(BF16)
    mix = jnp.dot(gated_ref[...], wout_ref[...], preferred_element_type=F32)
    x1_ref[0] = x + mod[2:3] * mix


def _spatial_gating(x, modv, norm_g, w_in, ln_g, ln_b, w_s, b_s, w_out):
    b, s, d = x.shape
    half = ln_g.shape[-1]
    tm = _token_tile(s, 256)
    assert tm % SG_CHUNK == 0
    tok = pl.BlockSpec((1, tm, d), lambda bi, si: (bi, si, 0))
    return pl.pallas_call(
        _sg_kernel,
        grid=(b, s // tm),
        in_specs=[
            tok,
            pl.BlockSpec((1, N_MOD, d), lambda bi, si: (bi, 0, 0)),
            _const_spec((1, d)),
            _const_spec((d, 2 * half)),
            _const_spec((1, half)),
            _const_spec((1, half)),
            _const_spec((SG_GROUPS, SG_CHUNK, SG_CHUNK)),
            _const_spec((SG_GROUPS, SG_CHUNK, 1)),
            _const_spec((half, d)),
        ],
        out_specs=tok,
        out_shape=jax.ShapeDtypeStruct((b, s, d), F32),
        scratch_shapes=[pltpu.VMEM((tm, half), BF16)],
        compiler_params=_params("parallel", "parallel"),
        name="spatial_gating",
    )(x, modv, norm_g.reshape(1, d), w_in.astype(BF16), ln_g.reshape(1, half), ln_b.reshape(1, half),
      w_s.astype(BF16), b_s.reshape(SG_GROUPS, SG_CHUNK, 1), w_out.astype(BF16))


def _first_argmax(cur, idx, sentinel):
    m = jnp.max(cur, axis=0, keepdims=True)
    first = jnp.min(jnp.where(cur == m, idx, sentinel), axis=0, keepdims=True)
    return m, idx == first


def _router_kernel(x_ref, mod_ref, g_ref, wrt_ref, bias_ref, h_ref, wd_ref, mk_ref):
    mod = mod_ref[0]
    h = _norm_mod(x_ref[0], g_ref[...], mod[3:4], mod[4:5])
    _rows_to_tiles(h_ref, h)
    logits = lax.dot_general(wrt_ref[...], h, (((1,), (1,)), ((), ())),
                             precision=HIGHEST, preferred_element_type=F32)
    scores = jax.nn.sigmoid(logits)
    sel = scores + bias_ref[...]
    tm = sel.shape[1]
    neg = -jnp.inf
    sub = lax.broadcasted_iota(jnp.int32, (GROUP_SIZE, tm), 0)
    rows = []
    for g in range(N_EXPERT_GROUPS):
        blk = sel[g * GROUP_SIZE:(g + 1) * GROUP_SIZE]
        m1, hit = _first_argmax(blk, sub, GROUP_SIZE)
        m2 = jnp.max(jnp.where(hit, neg, blk), axis=0, keepdims=True)
        rows.append(m1 + m2)
    cur = jnp.concatenate(rows, axis=0)
    gsel = jnp.zeros(cur.shape, jnp.int32)
    for _ in range(TOPK_GROUPS):
        _, hit = _first_argmax(cur, sub, N_EXPERT_GROUPS)
        gsel = jnp.where(hit, 1, gsel)
        cur = jnp.where(hit, neg, cur)
    masked = []
    for g in range(N_EXPERT_GROUPS):
        blk = sel[g * GROUP_SIZE:(g + 1) * GROUP_SIZE]
        masked.append(jnp.where(gsel[g:g + 1] > 0, blk, neg))
    cur = jnp.concatenate(masked, axis=0)
    eidx = lax.broadcasted_iota(jnp.int32, cur.shape, 0)
    chosen = jnp.zeros(cur.shape, jnp.int32)
    for _ in range(TOP_K):
        _, hit = _first_argmax(cur, eidx, N_EXPERTS)
        chosen = jnp.where(hit, 1, chosen)
        cur = jnp.where(hit, neg, cur)
    picked = jnp.where(chosen > 0, scores, 0.0)
    wsum = jnp.sum(picked, axis=0, keepdims=True)
    wd_ref[...] = picked / wsum * ROUTED_SCALE
    mk_ref[...] = chosen


def _router(x, modv, norm_g, w_router, router_bias):
    b, s, d = x.shape
    t = b * s
    e = w_router.shape[1]
    tm = _token_tile(s, 512)
    nst = s // tm
    tok = pl.BlockSpec((1, tm, d), lambda bi, si: (bi, si, 0))
    et = pl.BlockSpec((e, tm), lambda bi, si: (0, bi * nst + si))
    return pl.pallas_call(
        _router_kernel,
        grid=(b, nst),
        in_specs=[
            tok,
            pl.BlockSpec((1, N_MOD, d), lambda bi, si: (bi, 0, 0)),
            _const_spec((1, d)),
            _const_spec((e, d)),
            _const_spec((e, 1)),
        ],
        out_specs=[pl.BlockSpec((tm * (d // LANES), LANES), lambda bi, si: (bi * nst + si, 0)), et, et],
        out_shape=[jax.ShapeDtypeStruct((t * (d // LANES), LANES), F32),
                   jax.ShapeDtypeStruct((e, t), F32),
                   jax.ShapeDtypeStruct((e, t), jnp.int32)],
        compiler_params=_params("parallel", "parallel"),
        name="moe_router",
    )(x, modv, norm_g.reshape(1, d), w_router.T, router_bias.reshape(e, 1))


def _dispatch_plan(mask, wdense, n_blocks):
    e, t = mask.shape
    cnt = jnp.cumsum(mask, axis=1)
    counts = cnt[:, -1]
    padded = (counts + MOE_ROWS - 1) // MOE_ROWS * MOE_ROWS
    pad_end = jnp.cumsum(padded)
    pad_start = pad_end - padded
    dest_dense = pad_start[:, None] + cnt - mask
    ordinal = jnp.cumsum(mask, axis=0) - mask
    dest, wts = [], []
    for k in range(TOP_K):
        pick = (mask > 0) & (ordinal == k)
        dest.append(jnp.sum(jnp.where(pick, dest_dense, 0), axis=0))
        wts.append(jnp.sum(jnp.where(pick, wdense, 0.0), axis=0))
    dest = jnp.stack(dest, axis=1).astype(jnp.int32)
    wts = jnp.stack(wts, axis=1)
    blk_start = jnp.arange(n_blocks, dtype=jnp.int32) * MOE_ROWS
    block_e = jnp.minimum(jnp.sum(pad_end[None, :] <= blk_start[:, None], axis=1), e - 1).astype(jnp.int32)
    slack = padded - counts
    slack_end = jnp.cumsum(slack)
    q = jnp.arange(n_blocks * MOE_ROWS - t * TOP_K, dtype=jnp.int32)
    eq = jnp.sum(slack_end[None, :] <= q[:, None], axis=1)
    ec = jnp.minimum(eq, e - 1)
    in_expert = pad_start[ec] + counts[ec] + q - (slack_end[ec] - slack[ec])
    pad_rows = jnp.where(eq < e, in_expert, pad_end[-1] + q - slack_end[-1]).astype(jnp.int32)
    return dest, wts, block_e, pad_rows


TOKEN_TILE = 8


def _row_copy(src, dst, s_tok, d_tok, sem):
    s0 = pl.multiple_of(s_tok * TOKEN_TILE, TOKEN_TILE)
    d0 = pl.multiple_of(d_tok * TOKEN_TILE, TOKEN_TILE)
    return pltpu.make_async_copy(src.at[pl.ds(s0, TOKEN_TILE)], dst.at[pl.ds(d0, TOKEN_TILE)], sem)


def _drain(src, dst, sem, tokens, repeats):
    n = tokens * TOKEN_TILE
    for _ in range(repeats):
        pltpu.make_async_copy(src.at[pl.ds(0, n)], dst.at[pl.ds(0, n)], sem).wait()


def _dispatch_kernel(dest_ref, pads_ref, src_ref, zero_ref, dst_ref, sem, *, tokens, pads):
    base = pl.program_id(0) * tokens

    def body(i, carry):
        for k in range(TOP_K):
            _row_copy(src_ref, dst_ref, base + i, dest_ref[i * TOP_K + k], sem).start()
        return carry

    lax.fori_loop(0, tokens, body, 0)

    def fill(i, carry):
        _row_copy(zero_ref, dst_ref, 0, pads_ref[i], sem).start()
        return carry

    lax.fori_loop(0, pads, fill, 0)
    _drain(src_ref, dst_ref, sem, tokens, TOP_K)
    _drain(dst_ref, dst_ref, sem, pads, 1)


def _dispatch(dest_flat, pad_rows, src, out_tokens, tokens):
    t = src.shape[0] // TOKEN_TILE
    steps = t // tokens
    pads = pad_rows.shape[0] // steps
    assert pads * steps == pad_rows.shape[0]
    return pl.pallas_call(
        functools.partial(_dispatch_kernel, tokens=tokens, pads=pads),
        grid=(steps,),
        in_specs=[
            pl.BlockSpec((tokens * TOP_K,), lambda i: (i,), memory_space=pltpu.SMEM),
            pl.BlockSpec((pads,), lambda i: (i,), memory_space=pltpu.SMEM),
            pl.BlockSpec(memory_space=pl.ANY),
            pl.BlockSpec(memory_space=pl.ANY),
        ],
        out_specs=pl.BlockSpec(memory_space=pl.ANY),
        out_shape=jax.ShapeDtypeStruct((out_tokens * TOKEN_TILE, LANES), src.dtype),
        scratch_shapes=[pltpu.SemaphoreType.DMA(())],
        compiler_params=_params("arbitrary"),
        name="moe_dispatch",
    )(dest_flat, pad_rows, src, jnp.zeros((TOKEN_TILE, LANES), src.dtype))


def _collect_kernel(dest_ref, src_ref, dst_ref, sem, *, tokens, total):
    base = pl.program_id(0) * tokens

    def body(i, carry):
        for k in range(TOP_K):
            _row_copy(src_ref, dst_ref, dest_ref[i * TOP_K + k], k * total + base + i, sem).start()
        return carry

    lax.fori_loop(0, tokens, body, 0)
    _drain(src_ref, dst_ref, sem, tokens, TOP_K)


def _collect(dest_flat, src, tokens):
    t = dest_flat.shape[0] // TOP_K
    return pl.pallas_call(
        functools.partial(_collect_kernel, tokens=tokens, total=t),
        grid=(t // tokens,),
        in_specs=[
            pl.BlockSpec((tokens * TOP_K,), lambda i: (i,), memory_space=pltpu.SMEM),
            pl.BlockSpec(memory_space=pl.ANY),
        ],
        out_specs=pl.BlockSpec(memory_space=pl.ANY),
        out_shape=jax.ShapeDtypeStruct((TOP_K * t * TOKEN_TILE, LANES), src.dtype),
        scratch_shapes=[pltpu.SemaphoreType.DMA(())],
        compiler_params=_params("arbitrary"),
        name="moe_collect",
    )(dest_flat, src)


def _expert_kernel(be_ref, xs_ref, wg_ref, wu_ref, wd_ref, o_ref):
    xb = _rows_from_tiles(xs_ref, MOE_ROWS, TOKEN_TILE).astype(BF16)
    gate = jnp.dot(xb, wg_ref[0], preferred_element_type=F32)
    up = jnp.dot(xb, wu_ref[0], preferred_element_type=F32)
    hb = (gate * jax.nn.sigmoid(gate) * up).astype(BF16)
    _rows_to_tiles(o_ref, jnp.dot(hb, wd_ref[0], preferred_element_type=F32))


def _experts(xs, block_e, w_gate, w_up, w_down):
    d, f = w_gate.shape[1:]
    assert d == TOKEN_TILE * LANES
    nb = xs.shape[0] // (MOE_ROWS * TOKEN_TILE)
    rows = pl.BlockSpec((MOE_ROWS * TOKEN_TILE, LANES), lambda i, be: (i, 0))
    return pl.pallas_call(
        _expert_kernel,
        grid_spec=pltpu.PrefetchScalarGridSpec(
            num_scalar_prefetch=1,
            grid=(nb,),
            in_specs=[
                rows,
                pl.BlockSpec((1, d, f), lambda i, be: (be[i], 0, 0)),
                pl.BlockSpec((1, d, f), lambda i, be: (be[i], 0, 0)),
                pl.BlockSpec((1, f, d), lambda i, be: (be[i], 0, 0)),
            ],
            out_specs=rows,
        ),
        out_shape=jax.ShapeDtypeStruct(xs.shape, F32),
        compiler_params=_params("arbitrary"),
        name="moe_experts",
    )(block_e, xs, w_gate, w_up, w_down)


def _combine_kernel(x_ref, h_ref, eo_ref, w_ref, mod_ref, sg_ref, su_ref, sd_ref, fg_ref, o_ref,
                    *, final_norm):
    tm = x_ref.shape[1]
    hb = _rows_from_tiles(h_ref, tm, TOKEN_TILE).astype(BF16)
    gate = jnp.dot(hb, sg_ref[...], preferred_element_type=F32)
    up = jnp.dot(hb, su_ref[...], preferred_element_type=F32)
    y = jnp.dot((gate * jax.nn.sigmoid(gate) * up).astype(BF16), sd_ref[...], preferred_element_type=F32)
    w = w_ref[...]
    for k in range(TOP_K):
        y = y + _rows_from_tiles(eo_ref, tm, TOKEN_TILE, lead=(k,)) * w[:, k:k + 1]
    out = x_ref[0] + mod_ref[0][5:6] * y
    if final_norm:
        out = out * lax.rsqrt(jnp.mean(out * out, axis=-1, keepdims=True) + EPS) * fg_ref[...]
    o_ref[0] = out


def _combine(x, h, eo, wts, modv, ws_gate, ws_up, ws_down, final_g, final_norm):
    b, s, d = x.shape
    f = ws_gate.shape[-1]
    tm = _token_tile(s, 512)
    nst = s // tm
    tok = pl.BlockSpec((1, tm, d), lambda bi, si: (bi, si, 0))
    tr = tm * TOKEN_TILE
    return pl.pallas_call(
        functools.partial(_combine_kernel, final_norm=final_norm),
        grid=(b, nst),
        in_specs=[
            tok,
            pl.BlockSpec((tr, LANES), lambda bi, si: (bi * nst + si, 0)),
            pl.BlockSpec((TOP_K, tr, LANES), lambda bi, si: (0, bi * nst + si, 0)),
            pl.BlockSpec((tm, TOP_K), lambda bi, si: (bi * nst + si, 0)),
            pl.BlockSpec((1, N_MOD, d), lambda bi, si: (bi, 0, 0)),
            _const_spec((d, f)), _const_spec((d, f)), _const_spec((f, d)),
            _const_spec((1, d)),
        ],
        out_specs=tok,
        out_shape=jax.ShapeDtypeStruct((b, s, d), F32),
        compiler_params=_params("parallel", "parallel"),
        name="moe_combine",
    )(x, h, eo, wts, modv, ws_gate.astype(BF16), ws_up.astype(BF16), ws_down.astype(BF16),
      final_g.reshape(1, d))


def _moe_layer(x, modv, norm_g, w_router, router_bias, w_gate, w_up, w_down,
               ws_gate, ws_up, ws_down, final_g, final_norm):
    b, s, d = x.shape
    t = b * s
    nb = pl.cdiv(t * TOP_K, MOE_ROWS) + N_EXPERTS
    h, wdense, mask = _router(x, modv, norm_g, w_router, router_bias)
    dest, wts, block_e, pad_rows = _dispatch_plan(mask, wdense, nb)
    dest_flat = dest.reshape(t * TOP_K)
    tokens = _token_tile(t, 4096)
    xs = _dispatch(dest_flat, pad_rows, h, nb * MOE_ROWS, tokens)
    eo = _experts(xs, block_e, w_gate.astype(BF16), w_up.astype(BF16), w_down.astype(BF16))
    eo_tok = _collect(dest_flat, eo, tokens)
    return _combine(x, h, eo_tok.reshape(TOP_K, t * TOKEN_TILE, LANES), wts, modv,
                    ws_gate, ws_up, ws_down, final_g, final_norm)


def kernel(x, c, positions, norm1_g, norm2_g, ada_w, ada_b, da_w_in, da_lam_q1, da_lam_k1, da_lam_q2, da_lam_k2, da_subln_g, da_w_out, sg_w_in, sg_ln_g, sg_ln_b, sg_w_s, sg_b_s, sg_w_out, moe_w_router, moe_router_bias, moe_w_gate, moe_w_up, moe_w_down, moe_ws_gate, moe_ws_up, moe_ws_down, final_g):
    depth = ada_w.shape[0]
    mod = _ada_mod(c, ada_w, ada_b)
    for i in range(depth):
        j = i // 2
        if i % 2 == 0:
            lambda_init = 0.8 - 0.6 * math.exp(-0.3 * i)
            q, k, v = _qkv_rope(x, mod[i], norm1_g[i], positions, da_w_in[j].astype(BF16))
            lam_params = jnp.stack([da_lam_q1[j], da_lam_k1[j], da_lam_q2[j], da_lam_k2[j]])
            o = _diff_attention(q, k, v, lam_params, da_subln_g[j], lambda_init)
            x = _outproj_residual(o, x, mod[i], da_w_out[j].astype(BF16))
        else:
            x = _spatial_gating(x, mod[i], norm1_g[i], sg_w_in[j], sg_ln_g[j], sg_ln_b[j],
                                sg_w_s[j], sg_b_s[j], sg_w_out[j])
        x = _moe_layer(x, mod[i], norm2_g[i], moe_w_router[i], moe_router_bias[i],
                       moe_w_gate[i], moe_w_up[i], moe_w_down[i],
                       moe_ws_gate[i], moe_ws_up[i], moe_ws_down[i],
                       final_g, final_norm=(i == depth - 1))
    return x
```

```python
import functools
import math

import jax
import jax.numpy as jnp
from jax import lax
from jax.experimental import pallas as pl
from jax.experimental.pallas import tpu as pltpu

F32 = jnp.float32
BF16 = jnp.bfloat16
HIGHEST = lax.Precision.HIGHEST

EPS = 1e-6
LANES = 128
N_EXPERTS = 64
TOP_K = 6
N_EXPERT_GROUPS = 8
TOPK_GROUPS = 4
GROUP_SIZE = N_EXPERTS // N_EXPERT_GROUPS
ROUTED_SCALE = 2.5
DA_HEAD_DIM = 64
ROPE_THETA = 10000.0
SG_CHUNK = 128
SG_GROUPS = 8
N_MOD = 8

MOE_ROWS = 256
VMEM_LIMIT = 56 * 1024 * 1024


def _params(*sem):
    return pltpu.CompilerParams(dimension_semantics=sem, vmem_limit_bytes=VMEM_LIMIT)


def _const_spec(shape):
    n = len(shape)
    return pl.BlockSpec(shape, lambda *_: (0,) * n, pipeline_mode=pl.Buffered(1))


def _token_tile(s, target):
    t = min(s, target)
    assert s % t == 0
    return t


def _rows_to_tiles(ref, val):
    n, d = val.shape
    per = d // LANES
    for i in range(per):
        ref[pl.ds(i, n, stride=per), :] = val[:, i * LANES:(i + 1) * LANES]


def _rows_from_tiles(ref, n, per, lead=()):
    return jnp.concatenate([ref[lead + (pl.ds(i, n, stride=per), slice(None))] for i in range(per)],
                           axis=-1)


def _norm_mod(x, g, shift, scale):
    y = x * lax.rsqrt(jnp.mean(x * x, axis=-1, keepdims=True) + EPS) * g
    return y * (1.0 + scale) + shift


def _mod_kernel(c_ref, w_ref, b_ref, o_ref):
    c = c_ref[...]
    cond = c * jax.nn.sigmoid(c)
    o_ref[0] = jnp.dot(cond, w_ref[0], preferred_element_type=F32, precision=HIGHEST) + b_ref[0]


def _ada_mod(c, ada_w, ada_b):
    depth, d, d6 = ada_w.shape
    b = c.shape[0]
    nj = d6 // d
    out = pl.pallas_call(
        _mod_kernel,
        grid=(depth, nj),
        in_specs=[
            pl.BlockSpec((b, d), lambda i, j: (0, 0)),
            pl.BlockSpec((1, d, d), lambda i, j: (i, 0, j)),
            pl.BlockSpec((1, 1, d), lambda i, j: (i, 0, j)),
        ],
        out_specs=pl.BlockSpec((1, b, d), lambda i, j: (i, 0, j)),
        out_shape=jax.ShapeDtypeStruct((depth, b, d6), F32),
        compiler_params=_params("arbitrary", "arbitrary"),
        name="ada_mod",
    )(c, ada_w, ada_b.reshape(depth, 1, d6))
    mod = out.reshape(depth, b, nj, d)
    return jnp.pad(mod, ((0, 0), (0, 0), (0, N_MOD - nj), (0, 0)))


def _qkv_kernel(x_ref, mod_ref, g_ref, pos_ref, freq_ref, w_ref, q_ref, k_ref, v_ref):
    x = x_ref[0]
    mod = mod_ref[0]
    d = x.shape[-1]
    h = _norm_mod(x, g_ref[...], mod[0:1], mod[1:2])
    qkv = jnp.dot(h.astype(BF16), w_ref[...], preferred_element_type=F32)
    ang = pos_ref[0].astype(F32) * freq_ref[...]
    cos = jnp.cos(ang)
    sin = jnp.sin(ang)
    lane = lax.broadcasted_iota(jnp.int32, ang.shape, 1)
    first_half = (lane % DA_HEAD_DIM) < (DA_HEAD_DIM // 2)
    sin_signed = jnp.where(first_half, -sin, sin)
    half = DA_HEAD_DIM // 2

    def rope(blk):
        partner = jnp.where(first_half, pltpu.roll(blk, LANES - half, 1), pltpu.roll(blk, half, 1))
        return blk * cos + partner * sin_signed

    q_scale = DA_HEAD_DIM ** -0.5 * math.log2(math.e)
    for cb in range(d // LANES):
        lo = cb * LANES
        q_ref[0, :, lo:lo + LANES] = (rope(qkv[:, lo:lo + LANES]) * q_scale).astype(BF16)
        k_ref[0, :, lo:lo + LANES] = rope(qkv[:, d + lo:d + lo + LANES]).astype(BF16)
    v_ref[0] = qkv[:, 2 * d:].astype(BF16)


def _qkv_rope(x, modv, norm_g, positions, w_in_bf16):
    b, s, d = x.shape
    tm = _token_tile(s, 512)
    inv_freq = ROPE_THETA ** (-jnp.arange(0, DA_HEAD_DIM, 2, dtype=F32) / DA_HEAD_DIM)
    freq = jnp.tile(inv_freq, LANES // (DA_HEAD_DIM // 2)).reshape(1, LANES)
    tok = pl.BlockSpec((1, tm, d), lambda bi, si: (bi, si, 0))
    out = jax.ShapeDtypeStruct((b, s, d), BF16)
    return pl.pallas_call(
        _qkv_kernel,
        grid=(b, s // tm),
        in_specs=[
            tok,
            pl.BlockSpec((1, N_MOD, d), lambda bi, si: (bi, 0, 0)),
            _const_spec((1, d)),
            pl.BlockSpec((1, tm, 1), lambda bi, si: (bi, si, 0)),
            _const_spec((1, LANES)),
            _const_spec((d, 3 * d)),
        ],
        out_specs=[tok, tok, tok],
        out_shape=[out, out, out],
        compiler_params=_params("parallel", "parallel"),
        name="qkv_rope",
    )(x, modv, norm_g.reshape(1, d), positions.reshape(b, s, 1), freq, w_in_bf16)


ATTN_ROW_CHUNK = 64


def _attn_kernel(lam_ref, q_ref, k_ref, v_ref, g_ref, o_ref, qs_ref, s_ref, p_ref,
                 m_ref, l_ref, acc_ref, *, lambda_init, tq, tk):
    rows = 2 * tq
    q = q_ref[0]
    lane = lax.broadcasted_iota(jnp.int32, q.shape, 1)
    zero = jnp.zeros_like(q)
    qs_ref[0:tq] = jnp.where(lane < DA_HEAD_DIM, q, zero)
    qs_ref[tq:rows] = jnp.where(lane >= DA_HEAD_DIM, q, zero)
    m_ref[...] = jnp.full(m_ref.shape, -jnp.inf, F32)
    l_ref[...] = jnp.zeros(l_ref.shape, F32)
    acc_ref[...] = jnp.zeros(acc_ref.shape, F32)

    def step(tile, carry):
        off = pl.multiple_of(tile * tk, tk)
        s_ref[...] = lax.dot_general(qs_ref[...], k_ref[0, pl.ds(off, tk), :], (((1,), (1,)), ((), ())),
                                     preferred_element_type=F32)
        for r0 in range(0, rows, ATTN_ROW_CHUNK):
            rs = slice(r0, r0 + ATTN_ROW_CHUNK)
            mx = s_ref[rs, 0:LANES]
            for c0 in range(LANES, tk, LANES):
                mx = jnp.maximum(mx, s_ref[rs, c0:c0 + LANES])
            m_prev = m_ref[rs]
            m_new = jnp.maximum(m_prev, jnp.max(mx, axis=-1, keepdims=True))
            alpha = jnp.exp2(m_prev - m_new)
            psum = None
            for c0 in range(0, tk, LANES):
                p = jnp.exp2(s_ref[rs, c0:c0 + LANES] - m_new)
                psum = p if psum is None else psum + p
                p_ref[rs, c0:c0 + LANES] = p.astype(BF16)
            l_ref[rs] = alpha * l_ref[rs] + psum
            m_ref[rs] = m_new
            acc_ref[rs] = alpha * acc_ref[rs]
        acc_ref[...] += jnp.dot(p_ref[...], v_ref[0, pl.ds(off, tk), :], preferred_element_type=F32)
        return carry

    lax.fori_loop(0, k_ref.shape[1] // tk, step, 0)

    o = acc_ref[...] / jnp.sum(l_ref[...], axis=-1, keepdims=True)
    lp = lam_ref[...]
    lam = (jnp.exp(jnp.sum(lp[0:1] * lp[1:2], axis=-1, keepdims=True))
           - jnp.exp(jnp.sum(lp[2:3] * lp[3:4], axis=-1, keepdims=True)) + lambda_init)
    diff = o[0:tq] - lam * o[tq:2 * tq]
    y = diff * lax.rsqrt(jnp.mean(diff * diff, axis=-1, keepdims=True) + EPS) * g_ref[...]
    o_ref[0] = (y * (1.0 - lambda_init)).astype(BF16)


def _diff_attention(q, k, v, lam_params, subln_g, lambda_init):
    b, s, d = q.shape
    hw = 2 * DA_HEAD_DIM
    nh = d // hw
    tq = _token_tile(s, 256)
    tk = _token_tile(s, 1024)
    kern = functools.partial(_attn_kernel, lambda_init=lambda_init, tq=tq, tk=tk)
    return pl.pallas_call(
        kern,
        grid=(b, nh, s // tq),
        in_specs=[
            _const_spec((4, DA_HEAD_DIM)),
            pl.BlockSpec((1, tq, hw), lambda bi, hi, qi: (bi, qi, hi)),
            pl.BlockSpec((1, s, hw), lambda bi, hi, qi: (bi, 0, hi)),
            pl.BlockSpec((1, s, hw), lambda bi, hi, qi: (bi, 0, hi)),
            _const_spec((1, hw)),
        ],
        out_specs=pl.BlockSpec((1, tq, hw), lambda bi, hi, qi: (bi, qi, hi)),
        out_shape=jax.ShapeDtypeStruct((b, s, d), BF16),
        scratch_shapes=[
            pltpu.VMEM((2 * tq, hw), BF16),
            pltpu.VMEM((2 * tq, tk), F32),
            pltpu.VMEM((2 * tq, tk), BF16),
            pltpu.VMEM((2 * tq, LANES), F32),
            pltpu.VMEM((2 * tq, LANES), F32),
            pltpu.VMEM((2 * tq, hw), F32),
        ],
        compiler_params=_params("parallel", "parallel", "arbitrary"),
        name="diff_attn",
    )(lam_params, q, k, v, subln_g.reshape(1, hw))


def _outproj_kernel(o_ref, x_ref, mod_ref, w_ref, x1_ref):
    mix = jnp.dot(o_ref[0], w_ref[...], preferred_element_type=F32)
    x1_ref[0] = x_ref[0] + mod_ref[0][2:3] * mix


def _outproj_residual(o, x, modv, w_out_bf16):
    b, s, d = x.shape
    tm = _token_tile(s, 512)
    tok = pl.BlockSpec((1, tm, d), lambda bi, si: (bi, si, 0))
    return pl.pallas_call(
        _outproj_kernel,
        grid=(b, s // tm),
        in_specs=[tok, tok, pl.BlockSpec((1, N_MOD, d), lambda bi, si: (bi, 0, 0)), _const_spec((d, d))],
        out_specs=tok,
        out_shape=jax.ShapeDtypeStruct((b, s, d), F32),
        compiler_params=_params("parallel", "parallel"),
        name="attn_outproj",
    )(o, x, modv, w_out_bf16)


def _sg_kernel(x_ref, mod_ref, g_ref, win_ref, lng_ref, lnb_ref, ws_ref, bs_ref, wout_ref,
               x1_ref, gated_ref):
    x = x_ref[0]
    mod = mod_ref[0]
    tm = x.shape[0]
    half = lng_ref.shape[-1]
    gdim = half // SG_GROUPS
    h = _norm_mod(x, g_ref[...], mod[0:1], mod[1:2])
    z = jnp.dot(h.astype(BF16), win_ref[...], preferred_element_type=F32)
    z = 0.5 * z * (1.0 + lax.erf(z * (2.0 ** -0.5)))
    u = z[:, :half]
    v = z[:, half:]
    mu = jnp.mean(v, axis=-1, keepdims=True)
    vc = v - mu
    v = vc * lax.rsqrt(jnp.mean(vc * vc, axis=-1, keepdims=True) + EPS) * lng_ref[...] + lnb_ref[...]
    vb = v.astype(BF16)
    for c in range(tm // SG_CHUNK):
        r0 = c * SG_CHUNK
        for g in range(SG_GROUPS):
            c0 = g * gdim
            sp = jnp.dot(ws_ref[g], vb[r0:r0 + SG_CHUNK, c0:c0 + gdim],
                         preferred_element_type=F32) + bs_ref[g]
            gated_ref[r0:r0 + SG_CHUNK, c0:c0 + gdim] = (
                u[r0:r0 + SG_CHUNK, c0:c0 + gdim] * sp).astype(BF16)
    mix = jnp.dot(gated_ref[...], wout_ref[...], preferred_element_type=F32)
    x1_ref[0] = x + mod[2:3] * mix


def _spatial_gating(x, modv, norm_g, w_in, ln_g, ln_b, w_s, b_s, w_out):
    b, s, d = x.shape
    half = ln_g.shape[-1]
    tm = _token_tile(s, 256)
    assert tm % SG_CHUNK == 0
    tok = pl.BlockSpec((1, tm, d), lambda bi, si: (bi, si, 0))
    return pl.pallas_call(
        _sg_kernel,
        grid=(b, s // tm),
        in_specs=[
            tok,
            pl.BlockSpec((1, N_MOD, d), lambda bi, si: (bi, 0, 0)),
            _const_spec((1, d)),
            _const_spec((d, 2 * half)),
            _const_spec((1, half)),
            _const_spec((1, half)),
            _const_spec((SG_GROUPS, SG_CHUNK, SG_CHUNK)),
            _const_spec((SG_GROUPS, SG_CHUNK, 1)),
            _const_spec((half, d)),
        ],
        out_specs=tok,
        out_shape=jax.ShapeDtypeStruct((b, s, d), F32),
        scratch_shapes=[pltpu.VMEM((tm, half), BF16)],
        compiler_params=_params("parallel", "parallel"),
        name="spatial_gating",
    )(x, modv, norm_g.reshape(1, d), w_in.astype(BF16), ln_g.reshape(1, half), ln_b.reshape(1, half),
      w_s.astype(BF16), b_s.reshape(SG_GROUPS, SG_CHUNK, 1), w_out.astype(BF16))


def _first_argmax(cur, idx, sentinel):
    m = jnp.max(cur, axis=0, keepdims=True)
    first = jnp.min(jnp.where(cur == m, idx, sentinel), axis=0, keepdims=True)
    return m, idx == first


def _router_kernel(x_ref, mod_ref, g_ref, wrt_ref, bias_ref, h_ref, wd_ref, mk_ref):
    mod = mod_ref[0]
    h = _norm_mod(x_ref[0], g_ref[...], mod[3:4], mod[4:5])
    _rows_to_tiles(h_ref, h)
    logits = lax.dot_general(wrt_ref[...], h, (((1,), (1,)), ((), ())),
                             precision=HIGHEST, preferred_element_type=F32)
    scores = jax.nn.sigmoid(logits)
    sel = scores + bias_ref[...]
    tm = sel.shape[1]
    neg = -jnp.inf
    sub = lax.broadcasted_iota(jnp.int32, (GROUP_SIZE, tm), 0)
    rows = []
    for g in range(N_EXPERT_GROUPS):
        blk = sel[g * GROUP_SIZE:(g + 1) * GROUP_SIZE]
        m1, hit = _first_argmax(blk, sub, GROUP_SIZE)
        m2 = jnp.max(jnp.where(hit, neg, blk), axis=0, keepdims=True)
        rows.append(m1 + m2)
    cur = jnp.concatenate(rows, axis=0)
    gsel = jnp.zeros(cur.shape, jnp.int32)
    for _ in range(TOPK_GROUPS):
        _, hit = _first_argmax(cur, sub, N_EXPERT_GROUPS)
        gsel = jnp.where(hit, 1, gsel)
        cur = jnp.where(hit, neg, cur)
    masked = []
    for g in range(N_EXPERT_GROUPS):
        blk = sel[g * GROUP_SIZE:(g + 1) * GROUP_SIZE]
        masked.append(jnp.where(gsel[g:g + 1] > 0, blk, neg))
    cur = jnp.concatenate(masked, axis=0)
    eidx = lax.broadcasted_iota(jnp.int32, cur.shape, 0)
    chosen = jnp.zeros(cur.shape, jnp.int32)
    for _ in range(TOP_K):
        _, hit = _first_argmax(cur, eidx, N_EXPERTS)
        chosen = jnp.where(hit, 1, chosen)
        cur = jnp.where(hit, neg, cur)
    picked = jnp.where(chosen > 0, scores, 0.0)
    wsum = jnp.sum(picked, axis=0, keepdims=True)
    wd_ref[...] = picked / wsum * ROUTED_SCALE
    mk_ref[...] = chosen


def _router(x, modv, norm_g, w_router, router_bias):
    b, s, d = x.shape
    t = b * s
    e = w_router.shape[1]
    tm = _token_tile(s, 512)
    nst = s // tm
    tok = pl.BlockSpec((1, tm, d), lambda bi, si: (bi, si, 0))
    et = pl.BlockSpec((e, tm), lambda bi, si: (0, bi * nst + si))
    return pl.pallas_call(
        _router_kernel,
        grid=(b, nst),
        in_specs=[
            tok,
            pl.BlockSpec((1, N_MOD, d), lambda bi, si: (bi, 0, 0)),
            _const_spec((1, d)),
            _const_spec((e, d)),
            _const_spec((e, 1)),
        ],
        out_specs=[pl.BlockSpec((tm * (d // LANES), LANES), lambda bi, si: (bi * nst + si, 0)), et, et],
        out_shape=[jax.ShapeDtypeStruct((t * (d // LANES), LANES), F32),
                   jax.ShapeDtypeStruct((e, t), F32),
                   jax.ShapeDtypeStruct((e, t), jnp.int32)],
        compiler_params=_params("parallel", "parallel"),
        name="moe_router",
    )(x, modv, norm_g.reshape(1, d), w_router.T, router_bias.reshape(e, 1))


def _dispatch_plan(mask, wdense, n_blocks):
    e, t = mask.shape
    cnt = jnp.cumsum(mask, axis=1)
    counts = cnt[:, -1]
    padded = (counts + MOE_ROWS - 1) // MOE_ROWS * MOE_ROWS
    pad_end = jnp.cumsum(padded)
    pad_start = pad_end - padded
    dest_dense = pad_start[:, None] + cnt - mask
    ordinal = jnp.cumsum(mask, axis=0) - mask
    dest, wts = [], []
    for k in range(TOP_K):
        pick = (mask > 0) & (ordinal == k)
        dest.append(jnp.sum(jnp.where(pick, dest_dense, 0), axis=0))
        wts.append(jnp.sum(jnp.where(pick, wdense, 0.0), axis=0))
    dest = jnp.stack(dest, axis=1).astype(jnp.int32)
    wts = jnp.stack(wts, axis=1)
    blk_start = jnp.arange(n_blocks, dtype=jnp.int32) * MOE_ROWS
    block_e = jnp.minimum(jnp.sum(pad_end[None, :] <= blk_start[:, None], axis=1), e - 1).astype(jnp.int32)
    slack = padded - counts
    slack_end = jnp.cumsum(slack)
    q = jnp.arange(n_blocks * MOE_ROWS - t * TOP_K, dtype=jnp.int32)
    eq = jnp.sum(slack_end[None, :] <= q[:, None], axis=1)
    ec = jnp.minimum(eq, e - 1)
    in_expert = pad_start[ec] + counts[ec] + q - (slack_end[ec] - slack[ec])
    pad_rows = jnp.where(eq < e, in_expert, pad_end[-1] + q - slack_end[-1]).astype(jnp.int32)
    return dest, wts, block_e, pad_rows


TOKEN_TILE = 8


def _row_copy(src, dst, s_tok, d_tok, sem):
    s0 = pl.multiple_of(s_tok * TOKEN_TILE, TOKEN_TILE)
    d0 = pl.multiple_of(d_tok * TOKEN_TILE, TOKEN_TILE)
    return pltpu.make_async_copy(src.at[pl.ds(s0, TOKEN_TILE)], dst.at[pl.ds(d0, TOKEN_TILE)], sem)


def _drain(src, dst, sem, tokens, repeats):
    n = tokens * TOKEN_TILE
    for _ in range(repeats):
        pltpu.make_async_copy(src.at[pl.ds(0, n)], dst.at[pl.ds(0, n)], sem).wait()


def _dispatch_kernel(pads_ref, dest_ref, h_ref, xs_ref, zero_ref, sem, *, tokens, pads):
    def body(i, carry):
        for k in range(TOP_K):
            _row_copy(h_ref, xs_ref, i, dest_ref[i * TOP_K + k], sem).start()
        return carry

    lax.fori_loop(0, tokens, body, 0)
    zero_ref[...] = jnp.zeros(zero_ref.shape, zero_ref.dtype)
    pad0 = pl.program_id(0) * pads

    def fill(i, carry):
        _row_copy(zero_ref, xs_ref, 0, pads_ref[pad0 + i], sem).start()
        return carry

    lax.fori_loop(0, pads, fill, 0)
    _drain(h_ref, xs_ref, sem, tokens, TOP_K + pads // tokens)
    if pads % tokens:
        _drain(h_ref, xs_ref, sem, pads % tokens, 1)


def _dispatch(dest_flat, pad_rows, h, out_tokens, tokens):
    t = h.shape[0] // TOKEN_TILE
    steps = t // tokens
    pads = pad_rows.shape[0] // steps
    assert pads * steps == pad_rows.shape[0]
    return pl.pallas_call(
        functools.partial(_dispatch_kernel, tokens=tokens, pads=pads),
        grid_spec=pltpu.PrefetchScalarGridSpec(
            num_scalar_prefetch=1,
            grid=(steps,),
            in_specs=[
                pl.BlockSpec((tokens * TOP_K,), lambda i, pr: (i,), memory_space=pltpu.SMEM),
                pl.BlockSpec((tokens * TOKEN_TILE, LANES), lambda i, pr: (i, 0)),
            ],
            out_specs=pl.BlockSpec(memory_space=pl.ANY),
            scratch_shapes=[pltpu.VMEM((TOKEN_TILE, LANES), h.dtype), pltpu.SemaphoreType.DMA(())],
        ),
        out_shape=jax.ShapeDtypeStruct((out_tokens * TOKEN_TILE, LANES), h.dtype),
        compiler_params=_params("arbitrary"),
        name="moe_dispatch",
    )(pad_rows, dest_flat, h)


def _expert_kernel(be_ref, xs_ref, wg_ref, wu_ref, wd_ref, o_ref):
    xb = _rows_from_tiles(xs_ref, MOE_ROWS, TOKEN_TILE).astype(BF16)
    gate = jnp.dot(xb, wg_ref[0], preferred_element_type=F32)
    up = jnp.dot(xb, wu_ref[0], preferred_element_type=F32)
    hb = (gate * jax.nn.sigmoid(gate) * up).astype(BF16)
    _rows_to_tiles(o_ref, jnp.dot(hb, wd_ref[0], preferred_element_type=F32))


def _experts(xs, block_e, w_gate, w_up, w_down):
    d, f = w_gate.shape[1:]
    assert d == TOKEN_TILE * LANES
    nb = xs.shape[0] // (MOE_ROWS * TOKEN_TILE)
    rows = pl.BlockSpec((MOE_ROWS * TOKEN_TILE, LANES), lambda i, be: (i, 0))
    return pl.pallas_call(
        _expert_kernel,
        grid_spec=pltpu.PrefetchScalarGridSpec(
            num_scalar_prefetch=1,
            grid=(nb,),
            in_specs=[
                rows,
                pl.BlockSpec((1, d, f), lambda i, be: (be[i], 0, 0)),
                pl.BlockSpec((1, d, f), lambda i, be: (be[i], 0, 0)),
                pl.BlockSpec((1, f, d), lambda i, be: (be[i], 0, 0)),
            ],
            out_specs=rows,
        ),
        out_shape=jax.ShapeDtypeStruct(xs.shape, F32),
        compiler_params=_params("arbitrary"),
        name="moe_experts",
    )(block_e, xs, w_gate, w_up, w_down)


def _combine_kernel(dest_ref, x_ref, h_ref, eo_ref, w_ref, mod_ref, sg_ref, su_ref, sd_ref, fg_ref, o_ref,
                    buf_ref, sem, *, final_norm):
    tm = x_ref.shape[1]

    def gather(i, carry):
        for k in range(TOP_K):
            _row_copy(eo_ref, buf_ref.at[k], dest_ref[i * TOP_K + k], i, sem).start()
        return carry

    lax.fori_loop(0, tm, gather, 0)
    hb = _rows_from_tiles(h_ref, tm, TOKEN_TILE).astype(BF16)
    gate = jnp.dot(hb, sg_ref[...], preferred_element_type=F32)
    up = jnp.dot(hb, su_ref[...], preferred_element_type=F32)
    y = jnp.dot((gate * jax.nn.sigmoid(gate) * up).astype(BF16), sd_ref[...], preferred_element_type=F32)
    w = w_ref[...]
    _drain(eo_ref, buf_ref.at[0], sem, tm, TOP_K)
    for k in range(TOP_K):
        y = y + _rows_from_tiles(buf_ref, tm, TOKEN_TILE, lead=(k,)) * w[:, k:k + 1]
    out = x_ref[0] + mod_ref[0][5:6] * y
    if final_norm:
        out = out * lax.rsqrt(jnp.mean(out * out, axis=-1, keepdims=True) + EPS) * fg_ref[...]
    o_ref[0] = out


def _combine(dest_flat, x, h, eo, wts, modv, ws_gate, ws_up, ws_down, final_g, final_norm):
    b, s, d = x.shape
    f = ws_gate.shape[-1]
    tm = _token_tile(s, 512)
    nst = s // tm
    tok = pl.BlockSpec((1, tm, d), lambda bi, si: (bi, si, 0))
    tr = tm * TOKEN_TILE
    return pl.pallas_call(
        functools.partial(_combine_kernel, final_norm=final_norm),
        grid=(b, nst),
        in_specs=[
            pl.BlockSpec((tm * TOP_K,), lambda bi, si: (bi * nst + si,), memory_space=pltpu.SMEM),
            tok,
            pl.BlockSpec((tr, LANES), lambda bi, si: (bi * nst + si, 0)),
            pl.BlockSpec(memory_space=pl.ANY),
            pl.BlockSpec((tm, TOP_K), lambda bi, si: (bi * nst + si, 0)),
            pl.BlockSpec((1, N_MOD, d), lambda bi, si: (bi, 0, 0)),
            _const_spec((d, f)), _const_spec((d, f)), _const_spec((f, d)),
            _const_spec((1, d)),
        ],
        out_specs=tok,
        out_shape=jax.ShapeDtypeStruct((b, s, d), F32),
        scratch_shapes=[pltpu.VMEM((TOP_K, tr, LANES), F32), pltpu.SemaphoreType.DMA(())],
        compiler_params=_params("arbitrary", "arbitrary"),
        name="moe_combine",
    )(dest_flat, x, h, eo, wts, modv, ws_gate.astype(BF16), ws_up.astype(BF16), ws_down.astype(BF16),
      final_g.reshape(1, d))


def _moe_layer(x, modv, norm_g, w_router, router_bias, w_gate, w_up, w_down,
               ws_gate, ws_up, ws_down, final_g, final_norm):
    b, s, d = x.shape
    t = b * s
    nb = pl.cdiv(t * TOP_K, MOE_ROWS) + N_EXPERTS
    h, wdense, mask = _router(x, modv, norm_g, w_router, router_bias)
    dest, wts, block_e, pad_rows = _dispatch_plan(mask, wdense, nb)
    dest_flat = dest.reshape(t * TOP_K)
    xs = _dispatch(dest_flat, pad_rows, h, nb * MOE_ROWS, _token_tile(t, 512))
    eo = _experts(xs, block_e, w_gate.astype(BF16), w_up.astype(BF16), w_down.astype(BF16))
    return _combine(dest_flat, x, h, eo, wts, modv, ws_gate, ws_up, ws_down, final_g, final_norm)


def kernel(x, c, positions, norm1_g, norm2_g, ada_w, ada_b, da_w_in, da_lam_q1, da_lam_k1, da_lam_q2, da_lam_k2, da_subln_g, da_w_out, sg_w_in, sg_ln_g, sg_ln_b, sg_w_s, sg_b_s, sg_w_out, moe_w_router, moe_router_bias, moe_w_gate, moe_w_up, moe_w_down, moe_ws_gate, moe_ws_up, moe_ws_down, final_g):
    depth = ada_w.shape[0]
    mod = _ada_mod(c, ada_w, ada_b)
    for i in range(depth):
        j = i // 2
        if i % 2 == 0:
            lambda_init = 0.8 - 0.6 * math.exp(-0.3 * i)
            q, k, v = _qkv_rope(x, mod[i], norm1_g[i], positions, da_w_in[j].astype(BF16))
            lam_params = jnp.stack([da_lam_q1[j], da_lam_k1[j], da_lam_q2[j], da_lam_k2[j]])
            o = _diff_attention(q, k, v, lam_params, da_subln_g[j], lambda_init)
            x = _outproj_residual(o, x, mod[i], da_w_out[j].astype(BF16))
        else:
            x = _spatial_gating(x, mod[i], norm1_g[i], sg_w_in[j], sg_ln_g[j], sg_ln_b[j],
                                sg_w_s[j], sg_b_s[j], sg_w_out[j])
        x = _moe_layer(x, mod[i], norm2_g[i], moe_w_router[i], moe_router_bias[i],
                       moe_w_gate[i], moe_w_up[i], moe_w_down[i],
                       moe_ws_gate[i], moe_ws_up[i], moe_ws_down[i],
                       final_g, final_norm=(i == depth - 1))
    return x
```

```python
import functools
import math

import jax
import jax.numpy as jnp
from jax import lax
from jax.experimental import pallas as pl
from jax.experimental.pallas import tpu as pltpu

F32 = jnp.float32
BF16 = jnp.bfloat16
HIGHEST = lax.Precision.HIGHEST

EPS = 1e-6
LANES = 128
N_EXPERTS = 64
TOP_K = 6
N_EXPERT_GROUPS = 8
TOPK_GROUPS = 4
GROUP_SIZE = N_EXPERTS // N_EXPERT_GROUPS
ROUTED_SCALE = 2.5
DA_HEAD_DIM = 64
ROPE_THETA = 10000.0
SG_CHUNK = 128
SG_GROUPS = 8
N_MOD = 8

MOE_ROWS = 256
VMEM_LIMIT = 56 * 1024 * 1024


def _params(*sem):
    return pltpu.CompilerParams(dimension_semantics=sem, vmem_limit_bytes=VMEM_LIMIT)


def _const_spec(shape):
    n = len(shape)
    return pl.BlockSpec(shape, lambda *_: (0,) * n, pipeline_mode=pl.Buffered(1))


def _token_tile(s, target):
    t = min(s, target)
    assert s % t == 0
    return t


def _rows_to_tiles(ref, val):
    n, d = val.shape
    per = d // LANES
    for i in range(per):
        ref[pl.ds(i, n, stride=per), :] = val[:, i * LANES:(i + 1) * LANES]


def _rows_from_tiles(ref, n, per, lead=()):
    return jnp.concatenate([ref[lead + (pl.ds(i, n, stride=per), slice(None))] for i in range(per)],
                           axis=-1)


def _norm_mod(x, g, shift, scale):
    y = x * lax.rsqrt(jnp.mean(x * x, axis=-1, keepdims=True) + EPS) * g
    return y * (1.0 + scale) + shift


def _mod_kernel(c_ref, w_ref, b_ref, o_ref):
    c = c_ref[...]
    cond = c * jax.nn.sigmoid(c)
    o_ref[0] = jnp.dot(cond, w_ref[0], preferred_element_type=F32, precision=HIGHEST) + b_ref[0]


def _ada_mod(c, ada_w, ada_b):
    depth, d, d6 = ada_w.shape
    b = c.shape[0]
    nj = d6 // d
    out = pl.pallas_call(
        _mod_kernel,
        grid=(depth, nj),
        in_specs=[
            pl.BlockSpec((b, d), lambda i, j: (0, 0)),
            pl.BlockSpec((1, d, d), lambda i, j: (i, 0, j)),
            pl.BlockSpec((1, 1, d), lambda i, j: (i, 0, j)),
        ],
        out_specs=pl.BlockSpec((1, b, d), lambda i, j: (i, 0, j)),
        out_shape=jax.ShapeDtypeStruct((depth, b, d6), F32),
        compiler_params=_params("arbitrary", "arbitrary"),
        name="ada_mod",
    )(c, ada_w, ada_b.reshape(depth, 1, d6))
    mod = out.reshape(depth, b, nj, d)
    return jnp.pad(mod, ((0, 0), (0, 0), (0, N_MOD - nj), (0, 0)))


def _qkv_kernel(x_ref, mod_ref, g_ref, pos_ref, freq_ref, w_ref, q_ref, k_ref, v_ref):
    x = x_ref[0]
    mod = mod_ref[0]
    d = x.shape[-1]
    h = _norm_mod(x, g_ref[...], mod[0:1], mod[1:2])
    qkv = jnp.dot(h.astype(BF16), w_ref[...], preferred_element_type=F32)
    ang = pos_ref[0].astype(F32) * freq_ref[...]
    cos = jnp.cos(ang)
    sin = jnp.sin(ang)
    lane = lax.broadcasted_iota(jnp.int32, ang.shape, 1)
    first_half = (lane % DA_HEAD_DIM) < (DA_HEAD_DIM // 2)
    sin_signed = jnp.where(first_half, -sin, sin)
    half = DA_HEAD_DIM // 2

    def rope(blk):
        partner = jnp.where(first_half, pltpu.roll(blk, LANES - half, 1), pltpu.roll(blk, half, 1))
        return blk * cos + partner * sin_signed

    q_scale = DA_HEAD_DIM ** -0.5 * math.log2(math.e)
    for cb in range(d // LANES):
        lo = cb * LANES
        q_ref[0, :, lo:lo + LANES] = (rope(qkv[:, lo:lo + LANES]) * q_scale).astype(BF16)
        k_ref[0, :, lo:lo + LANES] = rope(qkv[:, d + lo:d + lo + LANES]).astype(BF16)
    v_ref[0] = qkv[:, 2 * d:].astype(BF16)


def _qkv_rope(x, modv, norm_g, positions, w_in_bf16):
    b, s, d = x.shape
    tm = _token_tile(s, 512)
    inv_freq = ROPE_THETA ** (-jnp.arange(0, DA_HEAD_DIM, 2, dtype=F32) / DA_HEAD_DIM)
    freq = jnp.tile(inv_freq, LANES // (DA_HEAD_DIM // 2)).reshape(1, LANES)
    tok = pl.BlockSpec((1, tm, d), lambda bi, si: (bi, si, 0))
    out = jax.ShapeDtypeStruct((b, s, d), BF16)
    return pl.pallas_call(
        _qkv_kernel,
        grid=(b, s // tm),
        in_specs=[
            tok,
            pl.BlockSpec((1, N_MOD, d), lambda bi, si: (bi, 0, 0)),
            _const_spec((1, d)),
            pl.BlockSpec((1, tm, 1), lambda bi, si: (bi, si, 0)),
            _const_spec((1, LANES)),
            _const_spec((d, 3 * d)),
        ],
        out_specs=[tok, tok, tok],
        out_shape=[out, out, out],
        compiler_params=_params("parallel", "parallel"),
        name="qkv_rope",
    )(x, modv, norm_g.reshape(1, d), positions.reshape(b, s, 1), freq, w_in_bf16)


ATTN_ROW_CHUNK = 64


ATTN_NORM_CHUNK = 1024
ATTN_SAFE_BOUND = 50.0


def _subhead_sq_norms(x):
    xf = x.astype(F32)
    sq = (xf * xf).astype(BF16)
    row = lax.broadcasted_iota(jnp.int32, (LANES, LANES), 0)
    first = (row < DA_HEAD_DIM).astype(BF16)
    second = (row >= DA_HEAD_DIM).astype(BF16)
    return (jnp.dot(sq, first, preferred_element_type=F32),
            jnp.dot(sq, second, preferred_element_type=F32))


def _attn_kernel(lam_ref, q_ref, k_ref, v_ref, g_ref, o_ref, qs_ref, s_ref, p_ref,
                 m_ref, l_ref, acc_ref, kmax_ref, *, lambda_init, tq, tk):
    rows = 2 * tq
    n_tiles = k_ref.shape[1] // tk

    @pl.when(pl.program_id(2) == 0)
    def _():
        chunk = min(ATTN_NORM_CHUNK, k_ref.shape[1])

        def body(c, carry):
            off = pl.multiple_of(c * chunk, chunk)
            n1, n2 = _subhead_sq_norms(k_ref[0, pl.ds(off, chunk), :])
            return (jnp.maximum(carry[0], jnp.max(n1, axis=0, keepdims=True)),
                    jnp.maximum(carry[1], jnp.max(n2, axis=0, keepdims=True)))

        zeros = jnp.zeros((1, LANES), F32)
        k1, k2 = lax.fori_loop(0, k_ref.shape[1] // chunk, body, (zeros, zeros))
        kmax_ref[0:1] = k1
        kmax_ref[1:2] = k2

    q = q_ref[0]
    lane = lax.broadcasted_iota(jnp.int32, q.shape, 1)
    zero = jnp.zeros_like(q)
    qs_ref[0:tq] = jnp.where(lane < DA_HEAD_DIM, q, zero)
    qs_ref[tq:rows] = jnp.where(lane >= DA_HEAD_DIM, q, zero)
    l_ref[...] = jnp.zeros(l_ref.shape, F32)
    acc_ref[...] = jnp.zeros(acc_ref.shape, F32)
    qn1, qn2 = _subhead_sq_norms(q)
    bound1 = jnp.sqrt(qn1 * kmax_ref[0:1])
    bound2 = jnp.sqrt(qn2 * kmax_ref[1:2])
    fixed_shift = jnp.maximum(jnp.max(bound1), jnp.max(bound2)) <= ATTN_SAFE_BOUND

    def scores(tile):
        off = pl.multiple_of(tile * tk, tk)
        return lax.dot_general(qs_ref[...], k_ref[0, pl.ds(off, tk), :], (((1,), (1,)), ((), ())),
                               preferred_element_type=F32)

    def add_pv(tile):
        off = pl.multiple_of(tile * tk, tk)
        acc_ref[...] += jnp.dot(p_ref[...], v_ref[0, pl.ds(off, tk), :], preferred_element_type=F32)

    @pl.when(fixed_shift)
    def _():
        m_ref[0:tq] = bound1
        m_ref[tq:rows] = bound2

        def step(tile, carry):
            s = scores(tile)
            for r0 in range(0, rows, ATTN_ROW_CHUNK):
                rs = slice(r0, r0 + ATTN_ROW_CHUNK)
                m = m_ref[rs]
                psum = None
                for c0 in range(0, tk, LANES):
                    p = jnp.exp2(s[rs, c0:c0 + LANES] - m)
                    psum = p if psum is None else psum + p
                    p_ref[rs, c0:c0 + LANES] = p.astype(BF16)
                l_ref[rs] += psum
            add_pv(tile)
            return carry

        lax.fori_loop(0, n_tiles, step, 0)

    @pl.when(jnp.logical_not(fixed_shift))
    def _():
        m_ref[...] = jnp.full(m_ref.shape, -jnp.inf, F32)

        def step(tile, carry):
            s_ref[...] = scores(tile)
            for r0 in range(0, rows, ATTN_ROW_CHUNK):
                rs = slice(r0, r0 + ATTN_ROW_CHUNK)
                mx = s_ref[rs, 0:LANES]
                for c0 in range(LANES, tk, LANES):
                    mx = jnp.maximum(mx, s_ref[rs, c0:c0 + LANES])
                m_prev = m_ref[rs]
                m_new = jnp.maximum(m_prev, jnp.max(mx, axis=-1, keepdims=True))
                alpha = jnp.exp2(m_prev - m_new)
                psum = None
                for c0 in range(0, tk, LANES):
                    p = jnp.exp2(s_ref[rs, c0:c0 + LANES] - m_new)
                    psum = p if psum is None else psum + p
                    p_ref[rs, c0:c0 + LANES] = p.astype(BF16)
                l_ref[rs] = alpha * l_ref[rs] + psum
                m_ref[rs] = m_new
                acc_ref[rs] = alpha * acc_ref[rs]
            add_pv(tile)
            return carry

        lax.fori_loop(0, n_tiles, step, 0)

    o = acc_ref[...] / jnp.sum(l_ref[...], axis=-1, keepdims=True)
    lp = lam_ref[...]
    lam = (jnp.exp(jnp.sum(lp[0:1] * lp[1:2], axis=-1, keepdims=True))
           - jnp.exp(jnp.sum(lp[2:3] * lp[3:4], axis=-1, keepdims=True)) + lambda_init)
    diff = o[0:tq] - lam * o[tq:2 * tq]
    y = diff * lax.rsqrt(jnp.mean(diff * diff, axis=-1, keepdims=True) + EPS) * g_ref[...]
    o_ref[0] = (y * (1.0 - lambda_init)).astype(BF16)


def _diff_attention(q, k, v, lam_params, subln_g, lambda_init):
    b, s, d = q.shape
    hw = 2 * DA_HEAD_DIM
    nh = d // hw
    tq = _token_tile(s, 512)
    tk = _token_tile(s, 2048)
    kern = functools.partial(_attn_kernel, lambda_init=lambda_init, tq=tq, tk=tk)
    return pl.pallas_call(
        kern,
        grid=(b, nh, s // tq),
        in_specs=[
            _const_spec((4, DA_HEAD_DIM)),
            pl.BlockSpec((1, tq, hw), lambda bi, hi, qi: (bi, qi, hi)),
            pl.BlockSpec((1, s, hw), lambda bi, hi, qi: (bi, 0, hi)),
            pl.BlockSpec((1, s, hw), lambda bi, hi, qi: (bi, 0, hi)),
            _const_spec((1, hw)),
        ],
        out_specs=pl.BlockSpec((1, tq, hw), lambda bi, hi, qi: (bi, qi, hi)),
        out_shape=jax.ShapeDtypeStruct((b, s, d), BF16),
        scratch_shapes=[
            pltpu.VMEM((2 * tq, hw), BF16),
            pltpu.VMEM((2 * tq, tk), F32),
            pltpu.VMEM((2 * tq, tk), BF16),
            pltpu.VMEM((2 * tq, LANES), F32),
            pltpu.VMEM((2 * tq, LANES), F32),
            pltpu.VMEM((2 * tq, hw), F32),
            pltpu.VMEM((8, LANES), F32),
        ],
        compiler_params=_params("parallel", "parallel", "arbitrary"),
        name="diff_attn",
    )(lam_params, q, k, v, subln_g.reshape(1, hw))


def _outproj_kernel(o_ref, x_ref, mod_ref, w_ref, x1_ref):
    mix = jnp.dot(o_ref[0], w_ref[...], preferred_element_type=F32)
    x1_ref[0] = x_ref[0] + mod_ref[0][2:3] * mix


def _outproj_residual(o, x, modv, w_out_bf16):
    b, s, d = x.shape
    tm = _token_tile(s, 512)
    tok = pl.BlockSpec((1, tm, d), lambda bi, si: (bi, si, 0))
    return pl.pallas_call(
        _outproj_kernel,
        grid=(b, s // tm),
        in_specs=[tok, tok, pl.BlockSpec((1, N_MOD, d), lambda bi, si: (bi, 0, 0)), _const_spec((d, d))],
        out_specs=tok,
        out_shape=jax.ShapeDtypeStruct((b, s, d), F32),
        compiler_params=_params("parallel", "parallel"),
        name="attn_outproj",
    )(o, x, modv, w_out_bf16)


def _sg_kernel(x_ref, mod_ref, g_ref, win_ref, lng_ref, lnb_ref, ws_ref, bs_ref, wout_ref,
               x1_ref, gated_ref):
    x = x_ref[0]
    mod = mod_ref[0]
    tm = x.shape[0]
    half = lng_ref.shape[-1]
    gdim = half // SG_GROUPS
    h = _norm_mod(x, g_ref[...], mod[0:1], mod[1:2])
    z = jnp.dot(h.astype(BF16), win_ref[...], preferred_element_type=F32)
    z = 0.5 * z * (1.0 + lax.erf(z * (2.0 ** -0.5)))
    u = z[:, :half]
    v = z[:, half:]
    mu = jnp.mean(v, axis=-1, keepdims=True)
    vc = v - mu
    v = vc * lax.rsqrt(jnp.mean(vc * vc, axis=-1, keepdims=True) + EPS) * lng_ref[...] + lnb_ref[...]
    vb = v.astype(BF16)
    for c in range(tm // SG_CHUNK):
        r0 = c * SG_CHUNK
        for g in range(SG_GROUPS):
            c0 = g * gdim
            sp = jnp.dot(ws_ref[g], vb[r0:r0 + SG_CHUNK, c0:c0 + gdim],
                         preferred_element_type=F32) + bs_ref[g]
            gated_ref[r0:r0 + SG_CHUNK, c0:c0 + gdim] = (
                u[r0:r0 + SG_CHUNK, c0:c0 + gdim] * sp).astype(BF16)
    mix = jnp.dot(gated_ref[...], wout_ref[...], preferred_element_type=F32)
    x1_ref[0] = x + mod[2:3] * mix


def _spatial_gating(x, modv, norm_g, w_in, ln_g, ln_b, w_s, b_s, w_out):
    b, s, d = x.shape
    half = ln_g.shape[-1]
    tm = _token_tile(s, 256)
    assert tm % SG_CHUNK == 0
    tok = pl.BlockSpec((1, tm, d), lambda bi, si: (bi, si, 0))
    return pl.pallas_call(
        _sg_kernel,
        grid=(b, s // tm),
        in_specs=[
            tok,
            pl.BlockSpec((1, N_MOD, d), lambda bi, si: (bi, 0, 0)),
            _const_spec((1, d)),
            _const_spec((d, 2 * half)),
            _const_spec((1, half)),
            _const_spec((1, half)),
            _const_spec((SG_GROUPS, SG_CHUNK, SG_CHUNK)),
            _const_spec((SG_GROUPS, SG_CHUNK, 1)),
            _const_spec((half, d)),
        ],
        out_specs=tok,
        out_shape=jax.ShapeDtypeStruct((b, s, d), F32),
        scratch_shapes=[pltpu.VMEM((tm, half), BF16)],
        compiler_params=_params("parallel", "parallel"),
        name="spatial_gating",
    )(x, modv, norm_g.reshape(1, d), w_in.astype(BF16), ln_g.reshape(1, half), ln_b.reshape(1, half),
      w_s.astype(BF16), b_s.reshape(SG_GROUPS, SG_CHUNK, 1), w_out.astype(BF16))


def _first_argmax(cur, idx, sentinel):
    m = jnp.max(cur, axis=0, keepdims=True)
    first = jnp.min(jnp.where(cur == m, idx, sentinel), axis=0, keepdims=True)
    return m, idx == first


def _router_kernel(x_ref, mod_ref, g_ref, wrt_ref, bias_ref, h_ref, wd_ref, mk_ref):
    mod = mod_ref[0]
    h = _norm_mod(x_ref[0], g_ref[...], mod[3:4], mod[4:5])
    _rows_to_tiles(h_ref, h)
    logits = lax.dot_general(wrt_ref[...], h, (((1,), (1,)), ((), ())),
                             precision=HIGHEST, preferred_element_type=F32)
    scores = jax.nn.sigmoid(logits)
    sel = scores + bias_ref[...]
    tm = sel.shape[1]
    neg = -jnp.inf
    sub = lax.broadcasted_iota(jnp.int32, (GROUP_SIZE, tm), 0)
    rows = []
    for g in range(N_EXPERT_GROUPS):
        blk = sel[g * GROUP_SIZE:(g + 1) * GROUP_SIZE]
        m1, hit = _first_argmax(blk, sub, GROUP_SIZE)
        m2 = jnp.max(jnp.where(hit, neg, blk), axis=0, keepdims=True)
        rows.append(m1 + m2)
    cur = jnp.concatenate(rows, axis=0)
    gsel = jnp.zeros(cur.shape, jnp.int32)
    for _ in range(TOPK_GROUPS):
        _, hit = _first_argmax(cur, sub, N_EXPERT_GROUPS)
        gsel = jnp.where(hit, 1, gsel)
        cur = jnp.where(hit, neg, cur)
    masked = []
    for g in range(N_EXPERT_GROUPS):
        blk = sel[g * GROUP_SIZE:(g + 1) * GROUP_SIZE]
        masked.append(jnp.where(gsel[g:g + 1] > 0, blk, neg))
    cur = jnp.concatenate(masked, axis=0)
    eidx = lax.broadcasted_iota(jnp.int32, cur.shape, 0)
    chosen = jnp.zeros(cur.shape, jnp.int32)
    for _ in range(TOP_K):
        _, hit = _first_argmax(cur, eidx, N_EXPERTS)
        chosen = jnp.where(hit, 1, chosen)
        cur = jnp.where(hit, neg, cur)
    picked = jnp.where(chosen > 0, scores, 0.0)
    wsum = jnp.sum(picked, axis=0, keepdims=True)
    wd_ref[...] = picked / wsum * ROUTED_SCALE
    mk_ref[...] = chosen


def _router(x, modv, norm_g, w_router, router_bias):
    b, s, d = x.shape
    t = b * s
    e = w_router.shape[1]
    tm = _token_tile(s, 512)
    nst = s // tm
    tok = pl.BlockSpec((1, tm, d), lambda bi, si: (bi, si, 0))
    et = pl.BlockSpec((e, tm), lambda bi, si: (0, bi * nst + si))
    return pl.pallas_call(
        _router_kernel,
        grid=(b, nst),
        in_specs=[
            tok,
            pl.BlockSpec((1, N_MOD, d), lambda bi, si: (bi, 0, 0)),
            _const_spec((1, d)),
            _const_spec((e, d)),
            _const_spec((e, 1)),
        ],
        out_specs=[pl.BlockSpec((tm * (d // LANES), LANES), lambda bi, si: (bi * nst + si, 0)), et, et],
        out_shape=[jax.ShapeDtypeStruct((t * (d // LANES), LANES), F32),
                   jax.ShapeDtypeStruct((e, t), F32),
                   jax.ShapeDtypeStruct((e, t), jnp.int32)],
        compiler_params=_params("parallel", "parallel"),
        name="moe_router",
    )(x, modv, norm_g.reshape(1, d), w_router.T, router_bias.reshape(e, 1))


def _dispatch_plan(mask, wdense, n_blocks):
    e, t = mask.shape
    cnt = jnp.cumsum(mask, axis=1)
    counts = cnt[:, -1]
    padded = (counts + MOE_ROWS - 1) // MOE_ROWS * MOE_ROWS
    pad_end = jnp.cumsum(padded)
    pad_start = pad_end - padded
    dest_dense = pad_start[:, None] + cnt - mask
    ordinal = jnp.cumsum(mask, axis=0) - mask
    dest, wts = [], []
    for k in range(TOP_K):
        pick = (mask > 0) & (ordinal == k)
        dest.append(jnp.sum(jnp.where(pick, dest_dense, 0), axis=0))
        wts.append(jnp.sum(jnp.where(pick, wdense, 0.0), axis=0))
    dest = jnp.stack(dest, axis=1).astype(jnp.int32)
    wts = jnp.stack(wts, axis=1)
    blk_start = jnp.arange(n_blocks, dtype=jnp.int32) * MOE_ROWS
    block_e = jnp.minimum(jnp.sum(pad_end[None, :] <= blk_start[:, None], axis=1), e - 1).astype(jnp.int32)
    slack = padded - counts
    slack_end = jnp.cumsum(slack)
    q = jnp.arange(n_blocks * MOE_ROWS - t * TOP_K, dtype=jnp.int32)
    eq = jnp.sum(slack_end[None, :] <= q[:, None], axis=1)
    ec = jnp.minimum(eq, e - 1)
    in_expert = pad_start[ec] + counts[ec] + q - (slack_end[ec] - slack[ec])
    pad_rows = jnp.where(eq < e, in_expert, pad_end[-1] + q - slack_end[-1]).astype(jnp.int32)
    return dest, wts, block_e, pad_rows


TOKEN_TILE = 8


def _row_copy(src, dst, s_tok, d_tok, sem):
    s0 = pl.multiple_of(s_tok * TOKEN_TILE, TOKEN_TILE)
    d0 = pl.multiple_of(d_tok * TOKEN_TILE, TOKEN_TILE)
    return pltpu.make_async_copy(src.at[pl.ds(s0, TOKEN_TILE)], dst.at[pl.ds(d0, TOKEN_TILE)], sem)


def _drain(src, dst, sem, tokens, repeats):
    n = tokens * TOKEN_TILE
    for _ in range(repeats):
        pltpu.make_async_copy(src.at[pl.ds(0, n)], dst.at[pl.ds(0, n)], sem).wait()


def _dispatch_kernel(pads_ref, dest_ref, h_ref, xs_ref, zero_ref, sem, *, tokens, pads):
    def body(i, carry):
        for k in range(TOP_K):
            _row_copy(h_ref, xs_ref, i, dest_ref[i * TOP_K + k], sem).start()
        return carry

    lax.fori_loop(0, tokens, body, 0)
    zero_ref[...] = jnp.zeros(zero_ref.shape, zero_ref.dtype)
    pad0 = pl.program_id(0) * pads

    def fill(i, carry):
        _row_copy(zero_ref, xs_ref, 0, pads_ref[pad0 + i], sem).start()
        return carry

    lax.fori_loop(0, pads, fill, 0)
    _drain(h_ref, xs_ref, sem, tokens, TOP_K + pads // tokens)
    if pads % tokens:
        _drain(h_ref, xs_ref, sem, pads % tokens, 1)


def _dispatch(dest_flat, pad_rows, h, out_tokens, tokens):
    t = h.shape[0] // TOKEN_TILE
    steps = t // tokens
    pads = pad_rows.shape[0] // steps
    assert pads * steps == pad_rows.shape[0]
    return pl.pallas_call(
        functools.partial(_dispatch_kernel, tokens=tokens, pads=pads),
        grid_spec=pltpu.PrefetchScalarGridSpec(
            num_scalar_prefetch=1,
            grid=(steps,),
            in_specs=[
                pl.BlockSpec((tokens * TOP_K,), lambda i, pr: (i,), memory_space=pltpu.SMEM),
                pl.BlockSpec((tokens * TOKEN_TILE, LANES), lambda i, pr: (i, 0)),
            ],
            out_specs=pl.BlockSpec(memory_space=pl.ANY),
            scratch_shapes=[pltpu.VMEM((TOKEN_TILE, LANES), h.dtype), pltpu.SemaphoreType.DMA(())],
        ),
        out_shape=jax.ShapeDtypeStruct((out_tokens * TOKEN_TILE, LANES), h.dtype),
        compiler_params=_params("arbitrary"),
        name="moe_dispatch",
    )(pad_rows, dest_flat, h)


def _expert_kernel(be_ref, xs_ref, wg_ref, wu_ref, wd_ref, o_ref):
    xb = _rows_from_tiles(xs_ref, MOE_ROWS, TOKEN_TILE).astype(BF16)
    gate = jnp.dot(xb, wg_ref[0], preferred_element_type=F32)
    up = jnp.dot(xb, wu_ref[0], preferred_element_type=F32)
    hb = (gate * jax.nn.sigmoid(gate) * up).astype(BF16)
    _rows_to_tiles(o_ref, jnp.dot(hb, wd_ref[0], preferred_element_type=F32))


def _experts(xs, block_e, w_gate, w_up, w_down):
    d, f = w_gate.shape[1:]
    assert d == TOKEN_TILE * LANES
    nb = xs.shape[0] // (MOE_ROWS * TOKEN_TILE)
    rows = pl.BlockSpec((MOE_ROWS * TOKEN_TILE, LANES), lambda i, be: (i, 0))
    return pl.pallas_call(
        _expert_kernel,
        grid_spec=pltpu.PrefetchScalarGridSpec(
            num_scalar_prefetch=1,
            grid=(nb,),
            in_specs=[
                rows,
                pl.BlockSpec((1, d, f), lambda i, be: (be[i], 0, 0)),
                pl.BlockSpec((1, d, f), lambda i, be: (be[i], 0, 0)),
                pl.BlockSpec((1, f, d), lambda i, be: (be[i], 0, 0)),
            ],
            out_specs=rows,
        ),
        out_shape=jax.ShapeDtypeStruct(xs.shape, F32),
        compiler_params=_params("arbitrary"),
        name="moe_experts",
    )(block_e, xs, w_gate, w_up, w_down)


def _combine_kernel(dest_ref, x_ref, h_ref, eo_ref, w_ref, mod_ref, sg_ref, su_ref, sd_ref, fg_ref, o_ref,
                    buf_ref, sem, *, final_norm):
    tm = x_ref.shape[1]

    def gather(i, carry):
        for k in range(TOP_K):
            _row_copy(eo_ref, buf_ref.at[k], dest_ref[i * TOP_K + k], i, sem).start()
        return carry

    lax.fori_loop(0, tm, gather, 0)
    hb = _rows_from_tiles(h_ref, tm, TOKEN_TILE).astype(BF16)
    gate = jnp.dot(hb, sg_ref[...], preferred_element_type=F32)
    up = jnp.dot(hb, su_ref[...], preferred_element_type=F32)
    y = jnp.dot((gate * jax.nn.sigmoid(gate) * up).astype(BF16), sd_ref[...], preferred_element_type=F32)
    w = w_ref[...]
    _drain(eo_ref, buf_ref.at[0], sem, tm, TOP_K)
    for k in range(TOP_K):
        y = y + _rows_from_tiles(buf_ref, tm, TOKEN_TILE, lead=(k,)) * w[:, k:k + 1]
    out = x_ref[0] + mod_ref[0][5:6] * y
    if final_norm:
        out = out * lax.rsqrt(jnp.mean(out * out, axis=-1, keepdims=True) + EPS) * fg_ref[...]
    o_ref[0] = out


def _combine(dest_flat, x, h, eo, wts, modv, ws_gate, ws_up, ws_down, final_g, final_norm):
    b, s, d = x.shape
    f = ws_gate.shape[-1]
    tm = _token_tile(s, 512)
    nst = s // tm
    tok = pl.BlockSpec((1, tm, d), lambda bi, si: (bi, si, 0))
    tr = tm * TOKEN_TILE
    return pl.pallas_call(
        functools.partial(_combine_kernel, final_norm=final_norm),
        grid=(b, nst),
        in_specs=[
            pl.BlockSpec((tm * TOP_K,), lambda bi, si: (bi * nst + si,), memory_space=pltpu.SMEM),
            tok,
            pl.BlockSpec((tr, LANES), lambda bi, si: (bi * nst + si, 0)),
            pl.BlockSpec(memory_space=pl.ANY),
            pl.BlockSpec((tm, TOP_K), lambda bi, si: (bi * nst + si, 0)),
            pl.BlockSpec((1, N_MOD, d), lambda bi, si: (bi, 0, 0)),
            _const_spec((d, f)), _const_spec((d, f)), _const_spec((f, d)),
            _const_spec((1, d)),
        ],
        out_specs=tok,
        out_shape=jax.ShapeDtypeStruct((b, s, d), F32),
        scratch_shapes=[pltpu.VMEM((TOP_K, tr, LANES), F32), pltpu.SemaphoreType.DMA(())],
        compiler_params=_params("arbitrary", "arbitrary"),
        name="moe_combine",
    )(dest_flat, x, h, eo, wts, modv, ws_gate.astype(BF16), ws_up.astype(BF16), ws_down.astype(BF16),
      final_g.reshape(1, d))


def _moe_layer(x, modv, norm_g, w_router, router_bias, w_gate, w_up, w_down,
               ws_gate, ws_up, ws_down, final_g, final_norm):
    b, s, d = x.shape
    t = b * s
    nb = pl.cdiv(t * TOP_K, MOE_ROWS) + N_EXPERTS
    h, wdense, mask = _router(x, modv, norm_g, w_router, router_bias)
    dest, wts, block_e, pad_rows = _dispatch_plan(mask, wdense, nb)
    dest_flat = dest.reshape(t * TOP_K)
    xs = _dispatch(dest_flat, pad_rows, h, nb * MOE_ROWS, _token_tile(t, 512))
    eo = _experts(xs, block_e, w_gate.astype(BF16), w_up.astype(BF16), w_down.astype(BF16))
    return _combine(dest_flat, x, h, eo, wts, modv, ws_gate, ws_up, ws_down, final_g, final_norm)


def kernel(x, c, positions, norm1_g, norm2_g, ada_w, ada_b, da_w_in, da_lam_q1, da_lam_k1, da_lam_q2, da_lam_k2, da_subln_g, da_w_out, sg_w_in, sg_ln_g, sg_ln_b, sg_w_s, sg_b_s, sg_w_out, moe_w_router, moe_router_bias, moe_w_gate, moe_w_up, moe_w_down, moe_ws_gate, moe_ws_up, moe_ws_down, final_g):
    depth = ada_w.shape[0]
    mod = _ada_mod(c, ada_w, ada_b)
    for i in range(depth):
        j = i // 2
        if i % 2 == 0:
            lambda_init = 0.8 - 0.6 * math.exp(-0.3 * i)
            q, k, v = _qkv_rope(x, mod[i], norm1_g[i], positions, da_w_in[j].astype(BF16))
            lam_params = jnp.stack([da_lam_q1[j], da_lam_k1[j], da_lam_q2[j], da_lam_k2[j]])
            o = _diff_attention(q, k, v, lam_params, da_subln_g[j], lambda_init)
            x = _outproj_residual(o, x, mod[i], da_w_out[j].astype(BF16))
        else:
            x = _spatial_gating(x, mod[i], norm1_g[i], sg_w_in[j], sg_ln_g[j], sg_ln_b[j],
                                sg_w_s[j], sg_b_s[j], sg_w_out[j])
        x = _moe_layer(x, mod[i], norm2_g[i], moe_w_router[i], moe_router_bias[i],
                       moe_w_gate[i], moe_w_up[i], moe_w_down[i],
                       moe_ws_gate[i], moe_ws_up[i], moe_ws_down[i],
                       final_g, final_norm=(i == depth - 1))
    return x
```

```python
import functools
import math

import jax
import jax.numpy as jnp
from jax import lax
from jax.experimental import pallas as pl
from jax.experimental.pallas import tpu as pltpu

F32 = jnp.float32
BF16 = jnp.bfloat16
HIGHEST = lax.Precision.HIGHEST

EPS = 1e-6
LANES = 128
N_EXPERTS = 64
TOP_K = 6
N_EXPERT_GROUPS = 8
TOPK_GROUPS = 4
GROUP_SIZE = N_EXPERTS // N_EXPERT_GROUPS
ROUTED_SCALE = 2.5
DA_HEAD_DIM = 64
ROPE_THETA = 10000.0
SG_CHUNK = 128
SG_GROUPS = 8
N_MOD = 8

MOE_ROWS = 512
VMEM_LIMIT = 56 * 1024 * 1024


def _params(*sem):
    return pltpu.CompilerParams(dimension_semantics=sem, vmem_limit_bytes=VMEM_LIMIT)


def _const_spec(shape):
    n = len(shape)
    return pl.BlockSpec(shape, lambda *_: (0,) * n, pipeline_mode=pl.Buffered(1))


def _token_tile(s, target):
    t = min(s, target)
    assert s % t == 0
    return t


def _rows_to_tiles(ref, val):
    n, d = val.shape
    per = d // LANES
    for i in range(per):
        ref[pl.ds(i, n, stride=per), :] = val[:, i * LANES:(i + 1) * LANES]


def _rows_from_tiles(ref, n, per, lead=()):
    return jnp.concatenate([ref[lead + (pl.ds(i, n, stride=per), slice(None))] for i in range(per)],
                           axis=-1)


def _norm_mod(x, g, shift, scale):
    y = x * lax.rsqrt(jnp.mean(x * x, axis=-1, keepdims=True) + EPS) * g
    return y * (1.0 + scale) + shift


def _mod_kernel(c_ref, w_ref, b_ref, o_ref):
    c = c_ref[...]
    cond = c * jax.nn.sigmoid(c)
    o_ref[0] = jnp.dot(cond, w_ref[0], preferred_element_type=F32, precision=HIGHEST) + b_ref[0]


def _ada_mod(c, ada_w, ada_b):
    depth, d, d6 = ada_w.shape
    b = c.shape[0]
    nj = d6 // d
    out = pl.pallas_call(
        _mod_kernel,
        grid=(depth, nj),
        in_specs=[
            pl.BlockSpec((b, d), lambda i, j: (0, 0)),
            pl.BlockSpec((1, d, d), lambda i, j: (i, 0, j)),
            pl.BlockSpec((1, 1, d), lambda i, j: (i, 0, j)),
        ],
        out_specs=pl.BlockSpec((1, b, d), lambda i, j: (i, 0, j)),
        out_shape=jax.ShapeDtypeStruct((depth, b, d6), F32),
        compiler_params=_params("arbitrary", "arbitrary"),
        name="ada_mod",
    )(c, ada_w, ada_b.reshape(depth, 1, d6))
    mod = out.reshape(depth, b, nj, d)
    return jnp.pad(mod, ((0, 0), (0, 0), (0, N_MOD - nj), (0, 0)))


def _qkv_kernel(x_ref, mod_ref, g_ref, pos_ref, freq_ref, w_ref, q_ref, k_ref, v_ref):
    x = x_ref[0]
    mod = mod_ref[0]
    d = x.shape[-1]
    h = _norm_mod(x, g_ref[...], mod[0:1], mod[1:2])
    qkv = jnp.dot(h.astype(BF16), w_ref[...], preferred_element_type=F32)
    ang = pos_ref[0].astype(F32) * freq_ref[...]
    cos = jnp.cos(ang)
    sin = jnp.sin(ang)
    lane = lax.broadcasted_iota(jnp.int32, ang.shape, 1)
    first_half = (lane % DA_HEAD_DIM) < (DA_HEAD_DIM // 2)
    sin_signed = jnp.where(first_half, -sin, sin)
    half = DA_HEAD_DIM // 2

    def rope(blk):
        partner = jnp.where(first_half, pltpu.roll(blk, LANES - half, 1), pltpu.roll(blk, half, 1))
        return blk * cos + partner * sin_signed

    q_scale = DA_HEAD_DIM ** -0.5 * math.log2(math.e)
    for cb in range(d // LANES):
        lo = cb * LANES
        q_ref[0, :, lo:lo + LANES] = (rope(qkv[:, lo:lo + LANES]) * q_scale).astype(BF16)
        k_ref[0, :, lo:lo + LANES] = rope(qkv[:, d + lo:d + lo + LANES]).astype(BF16)
    v_ref[0] = qkv[:, 2 * d:].astype(BF16)


def _qkv_rope(x, modv, norm_g, positions, w_in_bf16):
    b, s, d = x.shape
    tm = _token_tile(s, 512)
    inv_freq = ROPE_THETA ** (-jnp.arange(0, DA_HEAD_DIM, 2, dtype=F32) / DA_HEAD_DIM)
    freq = jnp.tile(inv_freq, LANES // (DA_HEAD_DIM // 2)).reshape(1, LANES)
    tok = pl.BlockSpec((1, tm, d), lambda bi, si: (bi, si, 0))
    out = jax.ShapeDtypeStruct((b, s, d), BF16)
    return pl.pallas_call(
        _qkv_kernel,
        grid=(b, s // tm),
        in_specs=[
            tok,
            pl.BlockSpec((1, N_MOD, d), lambda bi, si: (bi, 0, 0)),
            _const_spec((1, d)),
            pl.BlockSpec((1, tm, 1), lambda bi, si: (bi, si, 0)),
            _const_spec((1, LANES)),
            _const_spec((d, 3 * d)),
        ],
        out_specs=[tok, tok, tok],
        out_shape=[out, out, out],
        compiler_params=_params("parallel", "parallel"),
        name="qkv_rope",
    )(x, modv, norm_g.reshape(1, d), positions.reshape(b, s, 1), freq, w_in_bf16)


ATTN_ROW_CHUNK = 64


ATTN_NORM_CHUNK = 1024
ATTN_SAFE_BOUND = 50.0


def _subhead_sq_norms(x):
    xf = x.astype(F32)
    sq = (xf * xf).astype(BF16)
    row = lax.broadcasted_iota(jnp.int32, (LANES, LANES), 0)
    first = (row < DA_HEAD_DIM).astype(BF16)
    second = (row >= DA_HEAD_DIM).astype(BF16)
    return (jnp.dot(sq, first, preferred_element_type=F32),
            jnp.dot(sq, second, preferred_element_type=F32))


def _attn_kernel(lam_ref, q_ref, k_ref, v_ref, g_ref, o_ref, qs_ref, s_ref, p_ref,
                 m_ref, l_ref, acc_ref, kmax_ref, *, lambda_init, tq, tk):
    rows = 2 * tq
    n_tiles = k_ref.shape[1] // tk

    @pl.when(pl.program_id(2) == 0)
    def _():
        chunk = min(ATTN_NORM_CHUNK, k_ref.shape[1])

        def body(c, carry):
            off = pl.multiple_of(c * chunk, chunk)
            n1, n2 = _subhead_sq_norms(k_ref[0, pl.ds(off, chunk), :])
            return (jnp.maximum(carry[0], jnp.max(n1, axis=0, keepdims=True)),
                    jnp.maximum(carry[1], jnp.max(n2, axis=0, keepdims=True)))

        zeros = jnp.zeros((1, LANES), F32)
        k1, k2 = lax.fori_loop(0, k_ref.shape[1] // chunk, body, (zeros, zeros))
        kmax_ref[0:1] = k1
        kmax_ref[1:2] = k2

    q = q_ref[0]
    lane = lax.broadcasted_iota(jnp.int32, q.shape, 1)
    zero = jnp.zeros_like(q)
    qs_ref[0:tq] = jnp.where(lane < DA_HEAD_DIM, q, zero)
    qs_ref[tq:rows] = jnp.where(lane >= DA_HEAD_DIM, q, zero)
    l_ref[...] = jnp.zeros(l_ref.shape, F32)
    acc_ref[...] = jnp.zeros(acc_ref.shape, F32)
    qn1, qn2 = _subhead_sq_norms(q)
    bound1 = jnp.sqrt(qn1 * kmax_ref[0:1])
    bound2 = jnp.sqrt(qn2 * kmax_ref[1:2])
    fixed_shift = jnp.maximum(jnp.max(bound1), jnp.max(bound2)) <= ATTN_SAFE_BOUND

    def scores(tile):
        off = pl.multiple_of(tile * tk, tk)
        return lax.dot_general(qs_ref[...], k_ref[0, pl.ds(off, tk), :], (((1,), (1,)), ((), ())),
                               preferred_element_type=F32)

    def add_pv(tile):
        off = pl.multiple_of(tile * tk, tk)
        acc_ref[...] += jnp.dot(p_ref[...], v_ref[0, pl.ds(off, tk), :], preferred_element_type=F32)

    @pl.when(fixed_shift)
    def _():
        m_ref[0:tq] = bound1
        m_ref[tq:rows] = bound2

        def step(tile, carry):
            s = scores(tile)
            for r0 in range(0, rows, ATTN_ROW_CHUNK):
                rs = slice(r0, r0 + ATTN_ROW_CHUNK)
                m = m_ref[rs]
                psum = None
                for c0 in range(0, tk, LANES):
                    p = jnp.exp2(s[rs, c0:c0 + LANES] - m)
                    psum = p if psum is None else psum + p
                    p_ref[rs, c0:c0 + LANES] = p.astype(BF16)
                l_ref[rs] += psum
            add_pv(tile)
            return carry

        lax.fori_loop(0, n_tiles, step, 0)

    @pl.when(jnp.logical_not(fixed_shift))
    def _():
        m_ref[...] = jnp.full(m_ref.shape, -jnp.inf, F32)

        def step(tile, carry):
            s_ref[...] = scores(tile)
            for r0 in range(0, rows, ATTN_ROW_CHUNK):
                rs = slice(r0, r0 + ATTN_ROW_CHUNK)
                mx = s_ref[rs, 0:LANES]
                for c0 in range(LANES, tk, LANES):
                    mx = jnp.maximum(mx, s_ref[rs, c0:c0 + LANES])
                m_prev = m_ref[rs]
                m_new = jnp.maximum(m_prev, jnp.max(mx, axis=-1, keepdims=True))
                alpha = jnp.exp2(m_prev - m_new)
                psum = None
                for c0 in range(0, tk, LANES):
                    p = jnp.exp2(s_ref[rs, c0:c0 + LANES] - m_new)
                    psum = p if psum is None else psum + p
                    p_ref[rs, c0:c0 + LANES] = p.astype(BF16)
                l_ref[rs] = alpha * l_ref[rs] + psum
                m_ref[rs] = m_new
                acc_ref[rs] = alpha * acc_ref[rs]
            add_pv(tile)
            return carry

        lax.fori_loop(0, n_tiles, step, 0)

    o = acc_ref[...] / jnp.sum(l_ref[...], axis=-1, keepdims=True)
    lp = lam_ref[...]
    lam = (jnp.exp(jnp.sum(lp[0:1] * lp[1:2], axis=-1, keepdims=True))
           - jnp.exp(jnp.sum(lp[2:3] * lp[3:4], axis=-1, keepdims=True)) + lambda_init)
    diff = o[0:tq] - lam * o[tq:2 * tq]
    y = diff * lax.rsqrt(jnp.mean(diff * diff, axis=-1, keepdims=True) + EPS) * g_ref[...]
    o_ref[0] = (y * (1.0 - lambda_init)).astype(BF16)


def _diff_attention(q, k, v, lam_params, subln_g, lambda_init):
    b, s, d = q.shape
    hw = 2 * DA_HEAD_DIM
    nh = d // hw
    tq = _token_tile(s, 512)
    tk = _token_tile(s, 2048)
    kern = functools.partial(_attn_kernel, lambda_init=lambda_init, tq=tq, tk=tk)
    return pl.pallas_call(
        kern,
        grid=(b, nh, s // tq),
        in_specs=[
            _const_spec((4, DA_HEAD_DIM)),
            pl.BlockSpec((1, tq, hw), lambda bi, hi, qi: (bi, qi, hi)),
            pl.BlockSpec((1, s, hw), lambda bi, hi, qi: (bi, 0, hi)),
            pl.BlockSpec((1, s, hw), lambda bi, hi, qi: (bi, 0, hi)),
            _const_spec((1, hw)),
        ],
        out_specs=pl.BlockSpec((1, tq, hw), lambda bi, hi, qi: (bi, qi, hi)),
        out_shape=jax.ShapeDtypeStruct((b, s, d), BF16),
        scratch_shapes=[
            pltpu.VMEM((2 * tq, hw), BF16),
            pltpu.VMEM((2 * tq, tk), F32),
            pltpu.VMEM((2 * tq, tk), BF16),
            pltpu.VMEM((2 * tq, LANES), F32),
            pltpu.VMEM((2 * tq, LANES), F32),
            pltpu.VMEM((2 * tq, hw), F32),
            pltpu.VMEM((8, LANES), F32),
        ],
        compiler_params=_params("parallel", "parallel", "arbitrary"),
        name="diff_attn",
    )(lam_params, q, k, v, subln_g.reshape(1, hw))


def _outproj_kernel(o_ref, x_ref, mod_ref, w_ref, x1_ref):
    mix = jnp.dot(o_ref[0], w_ref[...], preferred_element_type=F32)
    x1_ref[0] = x_ref[0] + mod_ref[0][2:3] * mix


def _outproj_residual(o, x, modv, w_out_bf16):
    b, s, d = x.shape
    tm = _token_tile(s, 512)
    tok = pl.BlockSpec((1, tm, d), lambda bi, si: (bi, si, 0))
    return pl.pallas_call(
        _outproj_kernel,
        grid=(b, s // tm),
        in_specs=[tok, tok, pl.BlockSpec((1, N_MOD, d), lambda bi, si: (bi, 0, 0)), _const_spec((d, d))],
        out_specs=tok,
        out_shape=jax.ShapeDtypeStruct((b, s, d), F32),
        compiler_params=_params("parallel", "parallel"),
        name="attn_outproj",
    )(o, x, modv, w_out_bf16)


def _sg_kernel(x_ref, mod_ref, g_ref, win_ref, lng_ref, lnb_ref, ws_ref, bs_ref, wout_ref,
               x1_ref, gated_ref):
    x = x_ref[0]
    mod = mod_ref[0]
    tm = x.shape[0]
    half = lng_ref.shape[-1]
    gdim = half // SG_GROUPS
    h = _norm_mod(x, g_ref[...], mod[0:1], mod[1:2])
    z = jnp.dot(h.astype(BF16), win_ref[...], preferred_element_type=F32)
    z = 0.5 * z * (1.0 + lax.erf(z * (2.0 ** -0.5)))
    u = z[:, :half]
    v = z[:, half:]
    mu = jnp.mean(v, axis=-1, keepdims=True)
    vc = v - mu
    v = vc * lax.rsqrt(jnp.mean(vc * vc, axis=-1, keepdims=True) + EPS) * lng_ref[...] + lnb_ref[...]
    vb = v.astype(BF16)
    for c in range(tm // SG_CHUNK):
        r0 = c * SG_CHUNK
        for g in range(SG_GROUPS):
            c0 = g * gdim
            sp = jnp.dot(ws_ref[g], vb[r0:r0 + SG_CHUNK, c0:c0 + gdim],
                         preferred_element_type=F32) + bs_ref[g]
            gated_ref[r0:r0 + SG_CHUNK, c0:c0 + gdim] = (
                u[r0:r0 + SG_CHUNK, c0:c0 + gdim] * sp).astype(BF16)
    mix = jnp.dot(gated_ref[...], wout_ref[...], preferred_element_type=F32)
    x1_ref[0] = x + mod[2:3] * mix


def _spatial_gating(x, modv, norm_g, w_in, ln_g, ln_b, w_s, b_s, w_out):
    b, s, d = x.shape
    half = ln_g.shape[-1]
    tm = _token_tile(s, 256)
    assert tm % SG_CHUNK == 0
    tok = pl.BlockSpec((1, tm, d), lambda bi, si: (bi, si, 0))
    return pl.pallas_call(
        _sg_kernel,
        grid=(b, s // tm),
        in_specs=[
            tok,
            pl.BlockSpec((1, N_MOD, d), lambda bi, si: (bi, 0, 0)),
            _const_spec((1, d)),
            _const_spec((d, 2 * half)),
            _const_spec((1, half)),
            _const_spec((1, half)),
            _const_spec((SG_GROUPS, SG_CHUNK, SG_CHUNK)),
            _const_spec((SG_GROUPS, SG_CHUNK, 1)),
            _const_spec((half, d)),
        ],
        out_specs=tok,
        out_shape=jax.ShapeDtypeStruct((b, s, d), F32),
        scratch_shapes=[pltpu.VMEM((tm, half), BF16)],
        compiler_params=_params("parallel", "parallel"),
        name="spatial_gating",
    )(x, modv, norm_g.reshape(1, d), w_in.astype(BF16), ln_g.reshape(1, half), ln_b.reshape(1, half),
      w_s.astype(BF16), b_s.reshape(SG_GROUPS, SG_CHUNK, 1), w_out.astype(BF16))


def _first_argmax(cur, idx, sentinel):
    m = jnp.max(cur, axis=0, keepdims=True)
    first = jnp.min(jnp.where(cur == m, idx, sentinel), axis=0, keepdims=True)
    return m, idx == first


def _router_kernel(x_ref, mod_ref, g_ref, wrt_ref, bias_ref, h_ref, wd_ref, mk_ref):
    mod = mod_ref[0]
    h = _norm_mod(x_ref[0], g_ref[...], mod[3:4], mod[4:5])
    _rows_to_tiles(h_ref, h)
    logits = lax.dot_general(wrt_ref[...], h, (((1,), (1,)), ((), ())),
                             precision=HIGHEST, preferred_element_type=F32)
    scores = jax.nn.sigmoid(logits)
    sel = scores + bias_ref[...]
    tm = sel.shape[1]
    neg = -jnp.inf
    sub = lax.broadcasted_iota(jnp.int32, (GROUP_SIZE, tm), 0)
    rows = []
    for g in range(N_EXPERT_GROUPS):
        blk = sel[g * GROUP_SIZE:(g + 1) * GROUP_SIZE]
        m1, hit = _first_argmax(blk, sub, GROUP_SIZE)
        m2 = jnp.max(jnp.where(hit, neg, blk), axis=0, keepdims=True)
        rows.append(m1 + m2)
    cur = jnp.concatenate(rows, axis=0)
    gsel = jnp.zeros(cur.shape, jnp.int32)
    for _ in range(TOPK_GROUPS):
        _, hit = _first_argmax(cur, sub, N_EXPERT_GROUPS)
        gsel = jnp.where(hit, 1, gsel)
        cur = jnp.where(hit, neg, cur)
    masked = []
    for g in range(N_EXPERT_GROUPS):
        blk = sel[g * GROUP_SIZE:(g + 1) * GROUP_SIZE]
        masked.append(jnp.where(gsel[g:g + 1] > 0, blk, neg))
    cur = jnp.concatenate(masked, axis=0)
    eidx = lax.broadcasted_iota(jnp.int32, cur.shape, 0)
    chosen = jnp.zeros(cur.shape, jnp.int32)
    for _ in range(TOP_K):
        _, hit = _first_argmax(cur, eidx, N_EXPERTS)
        chosen = jnp.where(hit, 1, chosen)
        cur = jnp.where(hit, neg, cur)
    picked = jnp.where(chosen > 0, scores, 0.0)
    wsum = jnp.sum(picked, axis=0, keepdims=True)
    wd_ref[...] = picked / wsum * ROUTED_SCALE
    mk_ref[...] = chosen


def _router(x, modv, norm_g, w_router, router_bias):
    b, s, d = x.shape
    t = b * s
    e = w_router.shape[1]
    tm = _token_tile(s, 512)
    nst = s // tm
    tok = pl.BlockSpec((1, tm, d), lambda bi, si: (bi, si, 0))
    et = pl.BlockSpec((e, tm), lambda bi, si: (0, bi * nst + si))
    return pl.pallas_call(
        _router_kernel,
        grid=(b, nst),
        in_specs=[
            tok,
            pl.BlockSpec((1, N_MOD, d), lambda bi, si: (bi, 0, 0)),
            _const_spec((1, d)),
            _const_spec((e, d)),
            _const_spec((e, 1)),
        ],
        out_specs=[pl.BlockSpec((tm * (d // LANES), LANES), lambda bi, si: (bi * nst + si, 0)), et, et],
        out_shape=[jax.ShapeDtypeStruct((t * (d // LANES), LANES), F32),
                   jax.ShapeDtypeStruct((e, t), F32),
                   jax.ShapeDtypeStruct((e, t), jnp.int32)],
        compiler_params=_params("parallel", "parallel"),
        name="moe_router",
    )(x, modv, norm_g.reshape(1, d), w_router.T, router_bias.reshape(e, 1))


def _dispatch_plan(mask, wdense, n_blocks):
    e, t = mask.shape
    cnt = jnp.cumsum(mask, axis=1)
    counts = cnt[:, -1]
    padded = (counts + MOE_ROWS - 1) // MOE_ROWS * MOE_ROWS
    pad_end = jnp.cumsum(padded)
    pad_start = pad_end - padded
    dest_dense = pad_start[:, None] + cnt - mask
    ordinal = jnp.cumsum(mask, axis=0) - mask
    dest, wts = [], []
    for k in range(TOP_K):
        pick = (mask > 0) & (ordinal == k)
        dest.append(jnp.sum(jnp.where(pick, dest_dense, 0), axis=0))
        wts.append(jnp.sum(jnp.where(pick, wdense, 0.0), axis=0))
    dest = jnp.stack(dest, axis=1).astype(jnp.int32)
    wts = jnp.stack(wts, axis=1)
    blk_start = jnp.arange(n_blocks, dtype=jnp.int32) * MOE_ROWS
    block_e = jnp.minimum(jnp.sum(pad_end[None, :] <= blk_start[:, None], axis=1), e - 1).astype(jnp.int32)
    slack = padded - counts
    slack_end = jnp.cumsum(slack)
    q = jnp.arange(n_blocks * MOE_ROWS - t * TOP_K, dtype=jnp.int32)
    eq = jnp.sum(slack_end[None, :] <= q[:, None], axis=1)
    ec = jnp.minimum(eq, e - 1)
    in_expert = pad_start[ec] + counts[ec] + q - (slack_end[ec] - slack[ec])
    pad_rows = jnp.where(eq < e, in_expert, pad_end[-1] + q - slack_end[-1]).astype(jnp.int32)
    return dest, wts, block_e, pad_rows


TOKEN_TILE = 8


def _row_copy(src, dst, s_tok, d_tok, sem):
    s0 = pl.multiple_of(s_tok * TOKEN_TILE, TOKEN_TILE)
    d0 = pl.multiple_of(d_tok * TOKEN_TILE, TOKEN_TILE)
    return pltpu.make_async_copy(src.at[pl.ds(s0, TOKEN_TILE)], dst.at[pl.ds(d0, TOKEN_TILE)], sem)


def _drain(src, dst, sem, tokens, repeats):
    n = tokens * TOKEN_TILE
    for _ in range(repeats):
        pltpu.make_async_copy(src.at[pl.ds(0, n)], dst.at[pl.ds(0, n)], sem).wait()


def _dispatch_kernel(pads_ref, dest_ref, h_ref, xs_ref, zero_ref, sem, *, tokens, pads):
    def body(i, carry):
        for k in range(TOP_K):
            _row_copy(h_ref, xs_ref, i, dest_ref[i * TOP_K + k], sem).start(priority=k % 2)
        return carry

    lax.fori_loop(0, tokens, body, 0)
    zero_ref[...] = jnp.zeros(zero_ref.shape, zero_ref.dtype)
    pad0 = pl.program_id(0) * pads

    def fill(i, carry):
        _row_copy(zero_ref, xs_ref, 0, pads_ref[pad0 + i], sem).start()
        return carry

    lax.fori_loop(0, pads, fill, 0)
    _drain(h_ref, xs_ref, sem, tokens, TOP_K + pads // tokens)
    if pads % tokens:
        _drain(h_ref, xs_ref, sem, pads % tokens, 1)


def _dispatch(dest_flat, pad_rows, h, out_tokens, tokens):
    t = h.shape[0] // TOKEN_TILE
    steps = t // tokens
    pads = pad_rows.shape[0] // steps
    assert pads * steps == pad_rows.shape[0]
    return pl.pallas_call(
        functools.partial(_dispatch_kernel, tokens=tokens, pads=pads),
        grid_spec=pltpu.PrefetchScalarGridSpec(
            num_scalar_prefetch=1,
            grid=(steps,),
            in_specs=[
                pl.BlockSpec((tokens * TOP_K,), lambda i, pr: (i,), memory_space=pltpu.SMEM),
                pl.BlockSpec((tokens * TOKEN_TILE, LANES), lambda i, pr: (i, 0)),
            ],
            out_specs=pl.BlockSpec(memory_space=pl.ANY),
            scratch_shapes=[pltpu.VMEM((TOKEN_TILE, LANES), h.dtype), pltpu.SemaphoreType.DMA(())],
        ),
        out_shape=jax.ShapeDtypeStruct((out_tokens * TOKEN_TILE, LANES), h.dtype),
        compiler_params=_params("arbitrary"),
        name="moe_dispatch",
    )(pad_rows, dest_flat, h)


def _expert_kernel(be_ref, xs_ref, wg_ref, wu_ref, wd_ref, o_ref):
    xb = _rows_from_tiles(xs_ref, MOE_ROWS, TOKEN_TILE).astype(BF16)
    gate = jnp.dot(xb, wg_ref[0], preferred_element_type=F32)
    up = jnp.dot(xb, wu_ref[0], preferred_element_type=F32)
    hb = (gate * jax.nn.sigmoid(gate) * up).astype(BF16)
    _rows_to_tiles(o_ref, jnp.dot(hb, wd_ref[0], preferred_element_type=F32))


def _experts(xs, block_e, w_gate, w_up, w_down):
    d, f = w_gate.shape[1:]
    assert d == TOKEN_TILE * LANES
    nb = xs.shape[0] // (MOE_ROWS * TOKEN_TILE)
    rows = pl.BlockSpec((MOE_ROWS * TOKEN_TILE, LANES), lambda i, be: (i, 0))
    return pl.pallas_call(
        _expert_kernel,
        grid_spec=pltpu.PrefetchScalarGridSpec(
            num_scalar_prefetch=1,
            grid=(nb,),
            in_specs=[
                rows,
                pl.BlockSpec((1, d, f), lambda i, be: (be[i], 0, 0)),
                pl.BlockSpec((1, d, f), lambda i, be: (be[i], 0, 0)),
                pl.BlockSpec((1, f, d), lambda i, be: (be[i], 0, 0)),
            ],
            out_specs=rows,
        ),
        out_shape=jax.ShapeDtypeStruct(xs.shape, F32),
        compiler_params=_params("arbitrary"),
        name="moe_experts",
    )(block_e, xs, w_gate, w_up, w_down)


def _combine_kernel(dest_ref, x_ref, h_ref, eo_ref, w_ref, mod_ref, sg_ref, su_ref, sd_ref, fg_ref, o_ref,
                    buf_ref, sem, *, final_norm):
    tm = x_ref.shape[1]

    def gather(i, carry):
        for k in range(TOP_K):
            _row_copy(eo_ref, buf_ref.at[k], dest_ref[i * TOP_K + k], i, sem).start(priority=k % 2)
        return carry

    lax.fori_loop(0, tm, gather, 0)
    hb = _rows_from_tiles(h_ref, tm, TOKEN_TILE).astype(BF16)
    gate = jnp.dot(hb, sg_ref[...], preferred_element_type=F32)
    up = jnp.dot(hb, su_ref[...], preferred_element_type=F32)
    y = jnp.dot((gate * jax.nn.sigmoid(gate) * up).astype(BF16), sd_ref[...], preferred_element_type=F32)
    w = w_ref[...]
    _drain(eo_ref, buf_ref.at[0], sem, tm, TOP_K)
    for k in range(TOP_K):
        y = y + _rows_from_tiles(buf_ref, tm, TOKEN_TILE, lead=(k,)) * w[:, k:k + 1]
    out = x_ref[0] + mod_ref[0][5:6] * y
    if final_norm:
        out = out * lax.rsqrt(jnp.mean(out * out, axis=-1, keepdims=True) + EPS) * fg_ref[...]
    o_ref[0] = out


def _combine(dest_flat, x, h, eo, wts, modv, ws_gate, ws_up, ws_down, final_g, final_norm):
    b, s, d = x.shape
    f = ws_gate.shape[-1]
    tm = _token_tile(s, 512)
    nst = s // tm
    tok = pl.BlockSpec((1, tm, d), lambda bi, si: (bi, si, 0))
    tr = tm * TOKEN_TILE
    return pl.pallas_call(
        functools.partial(_combine_kernel, final_norm=final_norm),
        grid=(b, nst),
        in_specs=[
            pl.BlockSpec((tm * TOP_K,), lambda bi, si: (bi * nst + si,), memory_space=pltpu.SMEM),
            tok,
            pl.BlockSpec((tr, LANES), lambda bi, si: (bi * nst + si, 0)),
            pl.BlockSpec(memory_space=pl.ANY),
            pl.BlockSpec((tm, TOP_K), lambda bi, si: (bi * nst + si, 0)),
            pl.BlockSpec((1, N_MOD, d), lambda bi, si: (bi, 0, 0)),
            _const_spec((d, f)), _const_spec((d, f)), _const_spec((f, d)),
            _const_spec((1, d)),
        ],
        out_specs=tok,
        out_shape=jax.ShapeDtypeStruct((b, s, d), F32),
        scratch_shapes=[pltpu.VMEM((TOP_K, tr, LANES), F32), pltpu.SemaphoreType.DMA(())],
        compiler_params=_params("arbitrary", "arbitrary"),
        name="moe_combine",
    )(dest_flat, x, h, eo, wts, modv, ws_gate.astype(BF16), ws_up.astype(BF16), ws_down.astype(BF16),
      final_g.reshape(1, d))


def _moe_layer(x, modv, norm_g, w_router, router_bias, w_gate, w_up, w_down,
               ws_gate, ws_up, ws_down, final_g, final_norm):
    b, s, d = x.shape
    t = b * s
    nb = pl.cdiv(t * TOP_K, MOE_ROWS) + N_EXPERTS
    h, wdense, mask = _router(x, modv, norm_g, w_router, router_bias)
    dest, wts, block_e, pad_rows = _dispatch_plan(mask, wdense, nb)
    dest_flat = dest.reshape(t * TOP_K)
    xs = _dispatch(dest_flat, pad_rows, h, nb * MOE_ROWS, _token_tile(t, 512))
    eo = _experts(xs, block_e, w_gate.astype(BF16), w_up.astype(BF16), w_down.astype(BF16))
    return _combine(dest_flat, x, h, eo, wts, modv, ws_gate, ws_up, ws_down, final_g, final_norm)


def kernel(x, c, positions, norm1_g, norm2_g, ada_w, ada_b, da_w_in, da_lam_q1, da_lam_k1, da_lam_q2, da_lam_k2, da_subln_g, da_w_out, sg_w_in, sg_ln_g, sg_ln_b, sg_w_s, sg_b_s, sg_w_out, moe_w_router, moe_router_bias, moe_w_gate, moe_w_up, moe_w_down, moe_ws_gate, moe_ws_up, moe_ws_down, final_g):
    depth = ada_w.shape[0]
    mod = _ada_mod(c, ada_w, ada_b)
    for i in range(depth):
        j = i // 2
        if i % 2 == 0:
            lambda_init = 0.8 - 0.6 * math.exp(-0.3 * i)
            q, k, v = _qkv_rope(x, mod[i], norm1_g[i], positions, da_w_in[j].astype(BF16))
            lam_params = jnp.stack([da_lam_q1[j], da_lam_k1[j], da_lam_q2[j], da_lam_k2[j]])
            o = _diff_attention(q, k, v, lam_params, da_subln_g[j], lambda_init)
            x = _outproj_residual(o, x, mod[i], da_w_out[j].astype(BF16))
        else:
            x = _spatial_gating(x, mod[i], norm1_g[i], sg_w_in[j], sg_ln_g[j], sg_ln_b[j],
                                sg_w_s[j], sg_b_s[j], sg_w_out[j])
        x = _moe_layer(x, mod[i], norm2_g[i], moe_w_router[i], moe_router_bias[i],
                       moe_w_gate[i], moe_w_up[i], moe_w_down[i],
                       moe_ws_gate[i], moe_ws_up[i], moe_ws_down[i],
                       final_g, final_norm=(i == depth - 1))
    return x
```

```python
import functools
import math

import jax
import jax.numpy as jnp
from jax import lax
from jax.experimental import pallas as pl
from jax.experimental.pallas import tpu as pltpu

F32 = jnp.float32
BF16 = jnp.bfloat16
HIGHEST = lax.Precision.HIGHEST

EPS = 1e-6
LANES = 128
N_EXPERTS = 64
TOP_K = 6
N_EXPERT_GROUPS = 8
TOPK_GROUPS = 4
GROUP_SIZE = N_EXPERTS // N_EXPERT_GROUPS
ROUTED_SCALE = 2.5
DA_HEAD_DIM = 64
ROPE_THETA = 10000.0
SG_CHUNK = 128
SG_GROUPS = 8
N_MOD = 8

MOE_ROWS = 512
VMEM_LIMIT = 56 * 1024 * 1024


def _params(*sem):
    return pltpu.CompilerParams(dimension_semantics=sem, vmem_limit_bytes=VMEM_LIMIT)


def _const_spec(shape):
    n = len(shape)
    return pl.BlockSpec(shape, lambda *_: (0,) * n, pipeline_mode=pl.Buffered(1))


def _token_tile(s, target):
    t = min(s, target)
    assert s % t == 0
    return t


def _rows_to_tiles(ref, val):
    n, d = val.shape
    per = d // LANES
    for i in range(per):
        ref[pl.ds(i, n, stride=per), :] = val[:, i * LANES:(i + 1) * LANES]


def _rows_from_tiles(ref, n, per, lead=()):
    return jnp.concatenate([ref[lead + (pl.ds(i, n, stride=per), slice(None))] for i in range(per)],
                           axis=-1)


def _norm_mod(x, g, shift, scale):
    y = x * lax.rsqrt(jnp.mean(x * x, axis=-1, keepdims=True) + EPS) * g
    return y * (1.0 + scale) + shift


def _mod_kernel(c_ref, w_ref, b_ref, o_ref):
    c = c_ref[...]
    cond = c * jax.nn.sigmoid(c)
    o_ref[0] = jnp.dot(cond, w_ref[0], preferred_element_type=F32, precision=HIGHEST) + b_ref[0]


def _ada_mod(c, ada_w, ada_b):
    depth, d, d6 = ada_w.shape
    b = c.shape[0]
    nj = d6 // d
    out = pl.pallas_call(
        _mod_kernel,
        grid=(depth, nj),
        in_specs=[
            pl.BlockSpec((b, d), lambda i, j: (0, 0)),
            pl.BlockSpec((1, d, d), lambda i, j: (i, 0, j)),
            pl.BlockSpec((1, 1, d), lambda i, j: (i, 0, j)),
        ],
        out_specs=pl.BlockSpec((1, b, d), lambda i, j: (i, 0, j)),
        out_shape=jax.ShapeDtypeStruct((depth, b, d6), F32),
        compiler_params=_params("arbitrary", "arbitrary"),
        name="ada_mod",
    )(c, ada_w, ada_b.reshape(depth, 1, d6))
    mod = out.reshape(depth, b, nj, d)
    return jnp.pad(mod, ((0, 0), (0, 0), (0, N_MOD - nj), (0, 0)))


def _qkv_kernel(x_ref, mod_ref, g_ref, pos_ref, freq_ref, w_ref, q_ref, k_ref, v_ref):
    x = x_ref[0]
    mod = mod_ref[0]
    d = x.shape[-1]
    h = _norm_mod(x, g_ref[...], mod[0:1], mod[1:2])
    qkv = jnp.dot(h.astype(BF16), w_ref[...], preferred_element_type=F32)
    ang = pos_ref[0].astype(F32) * freq_ref[...]
    cos = jnp.cos(ang)
    sin = jnp.sin(ang)
    lane = lax.broadcasted_iota(jnp.int32, ang.shape, 1)
    first_half = (lane % DA_HEAD_DIM) < (DA_HEAD_DIM // 2)
    sin_signed = jnp.where(first_half, -sin, sin)
    half = DA_HEAD_DIM // 2

    def rope(blk):
        partner = jnp.where(first_half, pltpu.roll(blk, LANES - half, 1), pltpu.roll(blk, half, 1))
        return blk * cos + partner * sin_signed

    q_scale = DA_HEAD_DIM ** -0.5 * math.log2(math.e)
    for cb in range(d // LANES):
        lo = cb * LANES
        q_ref[0, :, lo:lo + LANES] = (rope(qkv[:, lo:lo + LANES]) * q_scale).astype(BF16)
        k_ref[0, :, lo:lo + LANES] = rope(qkv[:, d + lo:d + lo + LANES]).astype(BF16)
    v_ref[0] = qkv[:, 2 * d:].astype(BF16)


def _qkv_rope(x, modv, norm_g, positions, w_in_bf16):
    b, s, d = x.shape
    tm = _token_tile(s, 512)
    inv_freq = ROPE_THETA ** (-jnp.arange(0, DA_HEAD_DIM, 2, dtype=F32) / DA_HEAD_DIM)
    freq = jnp.tile(inv_freq, LANES // (DA_HEAD_DIM // 2)).reshape(1, LANES)
    tok = pl.BlockSpec((1, tm, d), lambda bi, si: (bi, si, 0))
    out = jax.ShapeDtypeStruct((b, s, d), BF16)
    return pl.pallas_call(
        _qkv_kernel,
        grid=(b, s // tm),
        in_specs=[
            tok,
            pl.BlockSpec((1, N_MOD, d), lambda bi, si: (bi, 0, 0)),
            _const_spec((1, d)),
            pl.BlockSpec((1, tm, 1), lambda bi, si: (bi, si, 0)),
            _const_spec((1, LANES)),
            _const_spec((d, 3 * d)),
        ],
        out_specs=[tok, tok, tok],
        out_shape=[out, out, out],
        compiler_params=_params("parallel", "parallel"),
        name="qkv_rope",
    )(x, modv, norm_g.reshape(1, d), positions.reshape(b, s, 1), freq, w_in_bf16)


ATTN_ROW_CHUNK = 64


ATTN_NORM_CHUNK = 1024
ATTN_SAFE_BOUND = 50.0


def _subhead_sq_norms(x):
    xf = x.astype(F32)
    sq = (xf * xf).astype(BF16)
    row = lax.broadcasted_iota(jnp.int32, (LANES, LANES), 0)
    first = (row < DA_HEAD_DIM).astype(BF16)
    second = (row >= DA_HEAD_DIM).astype(BF16)
    return (jnp.dot(sq, first, preferred_element_type=F32),
            jnp.dot(sq, second, preferred_element_type=F32))


def _attn_kernel(lam_ref, q_ref, k_ref, v_ref, g_ref, o_ref, qs_ref, s_ref, p_ref,
                 m_ref, l_ref, acc_ref, kmax_ref, *, lambda_init, tq, tk):
    rows = 2 * tq
    n_tiles = k_ref.shape[1] // tk

    @pl.when(pl.program_id(2) == 0)
    def _():
        chunk = min(ATTN_NORM_CHUNK, k_ref.shape[1])

        def body(c, carry):
            off = pl.multiple_of(c * chunk, chunk)
            n1, n2 = _subhead_sq_norms(k_ref[0, pl.ds(off, chunk), :])
            return (jnp.maximum(carry[0], jnp.max(n1, axis=0, keepdims=True)),
                    jnp.maximum(carry[1], jnp.max(n2, axis=0, keepdims=True)))

        zeros = jnp.zeros((1, LANES), F32)
        k1, k2 = lax.fori_loop(0, k_ref.shape[1] // chunk, body, (zeros, zeros))
        kmax_ref[0:1] = k1
        kmax_ref[1:2] = k2

    q = q_ref[0]
    lane = lax.broadcasted_iota(jnp.int32, q.shape, 1)
    zero = jnp.zeros_like(q)
    qs_ref[0:tq] = jnp.where(lane < DA_HEAD_DIM, q, zero)
    qs_ref[tq:rows] = jnp.where(lane >= DA_HEAD_DIM, q, zero)
    l_ref[...] = jnp.zeros(l_ref.shape, F32)
    acc_ref[...] = jnp.zeros(acc_ref.shape, F32)
    qn1, qn2 = _subhead_sq_norms(q)
    bound1 = jnp.sqrt(qn1 * kmax_ref[0:1])
    bound2 = jnp.sqrt(qn2 * kmax_ref[1:2])
    fixed_shift = jnp.maximum(jnp.max(bound1), jnp.max(bound2)) <= ATTN_SAFE_BOUND

    def scores(tile):
        off = pl.multiple_of(tile * tk, tk)
        return lax.dot_general(qs_ref[...], k_ref[0, pl.ds(off, tk), :], (((1,), (1,)), ((), ())),
                               preferred_element_type=F32)

    def add_pv(tile):
        off = pl.multiple_of(tile * tk, tk)
        acc_ref[...] += jnp.dot(p_ref[...], v_ref[0, pl.ds(off, tk), :], preferred_element_type=F32)

    @pl.when(fixed_shift)
    def _():
        m_ref[0:tq] = bound1
        m_ref[tq:rows] = bound2

        def step(tile, carry):
            s = scores(tile)
            for r0 in range(0, rows, ATTN_ROW_CHUNK):
                rs = slice(r0, r0 + ATTN_ROW_CHUNK)
                m = m_ref[rs]
                psum = None
                for c0 in range(0, tk, LANES):
                    p = jnp.exp2(s[rs, c0:c0 + LANES] - m)
                    psum = p if psum is None else psum + p
                    p_ref[rs, c0:c0 + LANES] = p.astype(BF16)
                l_ref[rs] += psum
            add_pv(tile)
            return carry

        lax.fori_loop(0, n_tiles, step, 0)

    @pl.when(jnp.logical_not(fixed_shift))
    def _():
        m_ref[...] = jnp.full(m_ref.shape, -jnp.inf, F32)

        def step(tile, carry):
            s_ref[...] = scores(tile)
            for r0 in range(0, rows, ATTN_ROW_CHUNK):
                rs = slice(r0, r0 + ATTN_ROW_CHUNK)
                mx = s_ref[rs, 0:LANES]
                for c0 in range(LANES, tk, LANES):
                    mx = jnp.maximum(mx, s_ref[rs, c0:c0 + LANES])
                m_prev = m_ref[rs]
                m_new = jnp.maximum(m_prev, jnp.max(mx, axis=-1, keepdims=True))
                alpha = jnp.exp2(m_prev - m_new)
                psum = None
                for c0 in range(0, tk, LANES):
                    p = jnp.exp2(s_ref[rs, c0:c0 + LANES] - m_new)
                    psum = p if psum is None else psum + p
                    p_ref[rs, c0:c0 + LANES] = p.astype(BF16)
                l_ref[rs] = alpha * l_ref[rs] + psum
                m_ref[rs] = m_new
                acc_ref[rs] = alpha * acc_ref[rs]
            add_pv(tile)
            return carry

        lax.fori_loop(0, n_tiles, step, 0)

    o = acc_ref[...] / jnp.sum(l_ref[...], axis=-1, keepdims=True)
    lp = lam_ref[...]
    lam = (jnp.exp(jnp.sum(lp[0:1] * lp[1:2], axis=-1, keepdims=True))
           - jnp.exp(jnp.sum(lp[2:3] * lp[3:4], axis=-1, keepdims=True)) + lambda_init)
    diff = o[0:tq] - lam * o[tq:2 * tq]
    y = diff * lax.rsqrt(jnp.mean(diff * diff, axis=-1, keepdims=True) + EPS) * g_ref[...]
    o_ref[0] = (y * (1.0 - lambda_init)).astype(BF16)


def _diff_attention(q, k, v, lam_params, subln_g, lambda_init):
    b, s, d = q.shape
    hw = 2 * DA_HEAD_DIM
    nh = d // hw
    tq = _token_tile(s, 1024)
    tk = _token_tile(s, 2048)
    kern = functools.partial(_attn_kernel, lambda_init=lambda_init, tq=tq, tk=tk)
    return pl.pallas_call(
        kern,
        grid=(b, nh, s // tq),
        in_specs=[
            _const_spec((4, DA_HEAD_DIM)),
            pl.BlockSpec((1, tq, hw), lambda bi, hi, qi: (bi, qi, hi)),
            pl.BlockSpec((1, s, hw), lambda bi, hi, qi: (bi, 0, hi)),
            pl.BlockSpec((1, s, hw), lambda bi, hi, qi: (bi, 0, hi)),
            _const_spec((1, hw)),
        ],
        out_specs=pl.BlockSpec((1, tq, hw), lambda bi, hi, qi: (bi, qi, hi)),
        out_shape=jax.ShapeDtypeStruct((b, s, d), BF16),
        scratch_shapes=[
            pltpu.VMEM((2 * tq, hw), BF16),
            pltpu.VMEM((2 * tq, tk), F32),
            pltpu.VMEM((2 * tq, tk), BF16),
            pltpu.VMEM((2 * tq, LANES), F32),
            pltpu.VMEM((2 * tq, LANES), F32),
            pltpu.VMEM((2 * tq, hw), F32),
            pltpu.VMEM((8, LANES), F32),
        ],
        compiler_params=_params("parallel", "parallel", "arbitrary"),
        name="diff_attn",
    )(lam_params, q, k, v, subln_g.reshape(1, hw))


def _outproj_kernel(o_ref, x_ref, mod_ref, w_ref, x1_ref):
    mix = jnp.dot(o_ref[0], w_ref[...], preferred_element_type=F32)
    x1_ref[0] = x_ref[0] + mod_ref[0][2:3] * mix


def _outproj_residual(o, x, modv, w_out_bf16):
    b, s, d = x.shape
    tm = _token_tile(s, 512)
    tok = pl.BlockSpec((1, tm, d), lambda bi, si: (bi, si, 0))
    return pl.pallas_call(
        _outproj_kernel,
        grid=(b, s // tm),
        in_specs=[tok, tok, pl.BlockSpec((1, N_MOD, d), lambda bi, si: (bi, 0, 0)), _const_spec((d, d))],
        out_specs=tok,
        out_shape=jax.ShapeDtypeStruct((b, s, d), F32),
        compiler_params=_params("parallel", "parallel"),
        name="attn_outproj",
    )(o, x, modv, w_out_bf16)


def _sg_kernel(x_ref, mod_ref, g_ref, win_ref, lng_ref, lnb_ref, ws_ref, bs_ref, wout_ref,
               x1_ref, gated_ref):
    x = x_ref[0]
    mod = mod_ref[0]
    tm = x.shape[0]
    half = lng_ref.shape[-1]
    gdim = half // SG_GROUPS
    h = _norm_mod(x, g_ref[...], mod[0:1], mod[1:2])
    z = jnp.dot(h.astype(BF16), win_ref[...], preferred_element_type=F32)
    z = 0.5 * z * (1.0 + lax.erf(z * (2.0 ** -0.5)))
    u = z[:, :half]
    v = z[:, half:]
    mu = jnp.mean(v, axis=-1, keepdims=True)
    vc = v - mu
    v = vc * lax.rsqrt(jnp.mean(vc * vc, axis=-1, keepdims=True) + EPS) * lng_ref[...] + lnb_ref[...]
    vb = v.astype(BF16)
    for c in range(tm // SG_CHUNK):
        r0 = c * SG_CHUNK
        for g in range(SG_GROUPS):
            c0 = g * gdim
            sp = jnp.dot(ws_ref[g], vb[r0:r0 + SG_CHUNK, c0:c0 + gdim],
                         preferred_element_type=F32) + bs_ref[g]
            gated_ref[r0:r0 + SG_CHUNK, c0:c0 + gdim] = (
                u[r0:r0 + SG_CHUNK, c0:c0 + gdim] * sp).astype(BF16)
    mix = jnp.dot(gated_ref[...], wout_ref[...], preferred_element_type=F32)
    x1_ref[0] = x + mod[2:3] * mix


def _spatial_gating(x, modv, norm_g, w_in, ln_g, ln_b, w_s, b_s, w_out):
    b, s, d = x.shape
    half = ln_g.shape[-1]
    tm = _token_tile(s, 256)
    assert tm % SG_CHUNK == 0
    tok = pl.BlockSpec((1, tm, d), lambda bi, si: (bi, si, 0))
    return pl.pallas_call(
        _sg_kernel,
        grid=(b, s // tm),
        in_specs=[
            tok,
            pl.BlockSpec((1, N_MOD, d), lambda bi, si: (bi, 0, 0)),
            _const_spec((1, d)),
            _const_spec((d, 2 * half)),
            _const_spec((1, half)),
            _const_spec((1, half)),
            _const_spec((SG_GROUPS, SG_CHUNK, SG_CHUNK)),
            _const_spec((SG_GROUPS, SG_CHUNK, 1)),
            _const_spec((half, d)),
        ],
        out_specs=tok,
        out_shape=jax.ShapeDtypeStruct((b, s, d), F32),
        scratch_shapes=[pltpu.VMEM((tm, half), BF16)],
        compiler_params=_params("parallel", "parallel"),
        name="spatial_gating",
    )(x, modv, norm_g.reshape(1, d), w_in.astype(BF16), ln_g.reshape(1, half), ln_b.reshape(1, half),
      w_s.astype(BF16), b_s.reshape(SG_GROUPS, SG_CHUNK, 1), w_out.astype(BF16))


def _first_argmax(cur, idx, sentinel):
    m = jnp.max(cur, axis=0, keepdims=True)
    first = jnp.min(jnp.where(cur == m, idx, sentinel), axis=0, keepdims=True)
    return m, idx == first


def _router_kernel(x_ref, mod_ref, g_ref, wrt_ref, bias_ref, upper_ref, lower_ref,
                   h_ref, rank_ref, eid_ref, w_ref, cnt_ref):
    mod = mod_ref[0]
    h = _norm_mod(x_ref[0], g_ref[...], mod[3:4], mod[4:5])
    _rows_to_tiles(h_ref, h)
    logits = lax.dot_general(wrt_ref[...], h, (((1,), (1,)), ((), ())),
                             precision=HIGHEST, preferred_element_type=F32)
    scores = jax.nn.sigmoid(logits)
    sel = scores + bias_ref[...]
    tm = sel.shape[1]
    neg = -jnp.inf
    sub = lax.broadcasted_iota(jnp.int32, (GROUP_SIZE, tm), 0)
    rows = []
    for g in range(N_EXPERT_GROUPS):
        blk = sel[g * GROUP_SIZE:(g + 1) * GROUP_SIZE]
        m1, hit = _first_argmax(blk, sub, GROUP_SIZE)
        m2 = jnp.max(jnp.where(hit, neg, blk), axis=0, keepdims=True)
        rows.append(m1 + m2)
    cur = jnp.concatenate(rows, axis=0)
    gsel = jnp.zeros(cur.shape, jnp.int32)
    for _ in range(TOPK_GROUPS):
        _, hit = _first_argmax(cur, sub, N_EXPERT_GROUPS)
        gsel = jnp.where(hit, 1, gsel)
        cur = jnp.where(hit, neg, cur)
    masked = []
    for g in range(N_EXPERT_GROUPS):
        blk = sel[g * GROUP_SIZE:(g + 1) * GROUP_SIZE]
        masked.append(jnp.where(gsel[g:g + 1] > 0, blk, neg))
    cur = jnp.concatenate(masked, axis=0)
    eidx = lax.broadcasted_iota(jnp.int32, cur.shape, 0)
    chosen = jnp.zeros(cur.shape, jnp.int32)
    for _ in range(TOP_K):
        _, hit = _first_argmax(cur, eidx, N_EXPERTS)
        chosen = jnp.where(hit, 1, chosen)
        cur = jnp.where(hit, neg, cur)
    picked = jnp.where(chosen > 0, scores, 0.0)
    wsum = jnp.sum(picked, axis=0, keepdims=True)
    wdense = picked / wsum * ROUTED_SCALE

    @pl.when((pl.program_id(0) == 0) & (pl.program_id(1) == 0))
    def _():
        cnt_ref[...] = jnp.zeros(cnt_ref.shape, F32)

    cm = chosen.astype(BF16)
    base = cnt_ref[...]
    rank = (jnp.dot(cm, upper_ref[...], preferred_element_type=F32)
            + jnp.concatenate([base] * (tm // LANES), axis=1))
    cnt_ref[...] = base + jnp.dot(cm, jnp.ones((tm, LANES), BF16), preferred_element_type=F32)
    ordinal = jnp.dot(lower_ref[...], cm, preferred_element_type=F32)
    eid = eidx.astype(F32)
    rows_r, rows_e, rows_w = [], [], []
    for k in range(TOP_K):
        pick = (chosen > 0) & (ordinal == k)
        rows_r.append(jnp.sum(jnp.where(pick, rank, 0.0), axis=0, keepdims=True))
        rows_e.append(jnp.sum(jnp.where(pick, eid, 0.0), axis=0, keepdims=True))
        rows_w.append(jnp.sum(jnp.where(pick, wdense, 0.0), axis=0, keepdims=True))
    fill = [jnp.zeros((N_MOD - TOP_K, tm), F32)]
    rank_ref[...] = jnp.concatenate(rows_r + fill, axis=0).astype(jnp.int32)
    eid_ref[...] = jnp.concatenate(rows_e + fill, axis=0).astype(jnp.int32)
    w_ref[...] = jnp.concatenate(rows_w + fill, axis=0)


def _router(x, modv, norm_g, w_router, router_bias, tm):
    b, s, d = x.shape
    t = b * s
    e = w_router.shape[1]
    nst = s // tm
    tok = pl.BlockSpec((1, tm, d), lambda bi, si: (bi, si, 0))
    per_k = pl.BlockSpec((N_MOD, tm), lambda bi, si: (0, bi * nst + si))
    upper = (jnp.arange(tm)[:, None] < jnp.arange(tm)[None, :]).astype(BF16)
    lower = (jnp.arange(e)[None, :] < jnp.arange(e)[:, None]).astype(BF16)
    return pl.pallas_call(
        _router_kernel,
        grid=(b, nst),
        in_specs=[
            tok,
            pl.BlockSpec((1, N_MOD, d), lambda bi, si: (bi, 0, 0)),
            _const_spec((1, d)),
            _const_spec((e, d)),
            _const_spec((e, 1)),
            _const_spec((tm, tm)),
            _const_spec((e, e)),
        ],
        out_specs=[pl.BlockSpec((tm * (d // LANES), LANES), lambda bi, si: (bi * nst + si, 0)),
                   per_k, per_k, per_k,
                   pl.BlockSpec((e, LANES), lambda bi, si: (0, 0))],
        out_shape=[jax.ShapeDtypeStruct((t * (d // LANES), LANES), F32),
                   jax.ShapeDtypeStruct((N_MOD, t), jnp.int32),
                   jax.ShapeDtypeStruct((N_MOD, t), jnp.int32),
                   jax.ShapeDtypeStruct((N_MOD, t), F32),
                   jax.ShapeDtypeStruct((e, LANES), F32)],
        compiler_params=_params("arbitrary", "arbitrary"),
        name="moe_router",
    )(x, modv, norm_g.reshape(1, d), w_router.T, router_bias.reshape(e, 1), upper, lower)


def _dispatch_plan(rank, eid, counts, n_blocks, tile):
    e = counts.shape[0]
    t = rank.shape[1]
    padded = (counts + MOE_ROWS - 1) // MOE_ROWS * MOE_ROWS
    pad_end = jnp.cumsum(padded)
    pad_start = pad_end - padded
    start_of = jnp.sum(jnp.where(eid[:, :, None] == jnp.arange(e)[None, None, :], pad_start[None, None, :], 0),
                       axis=-1)
    dest = (rank + start_of).astype(jnp.int32)
    dest = dest.reshape(TOP_K, t // tile, tile).transpose(1, 0, 2).reshape(t * TOP_K)
    blk_start = jnp.arange(n_blocks, dtype=jnp.int32) * MOE_ROWS
    block_e = jnp.minimum(jnp.sum(pad_end[None, :] <= blk_start[:, None], axis=1), e - 1).astype(jnp.int32)
    slack = padded - counts
    slack_end = jnp.cumsum(slack)
    q = jnp.arange(n_blocks * MOE_ROWS - t * TOP_K, dtype=jnp.int32)
    eq = jnp.sum(slack_end[None, :] <= q[:, None], axis=1)
    ec = jnp.minimum(eq, e - 1)
    in_expert = pad_start[ec] + counts[ec] + q - (slack_end[ec] - slack[ec])
    pad_rows = jnp.where(eq < e, in_expert, pad_end[-1] + q - slack_end[-1]).astype(jnp.int32)
    return dest, block_e, pad_rows


TOKEN_TILE = 8


def _row_copy(src, dst, s_tok, d_tok, sem):
    s0 = pl.multiple_of(s_tok * TOKEN_TILE, TOKEN_TILE)
    d0 = pl.multiple_of(d_tok * TOKEN_TILE, TOKEN_TILE)
    return pltpu.make_async_copy(src.at[pl.ds(s0, TOKEN_TILE)], dst.at[pl.ds(d0, TOKEN_TILE)], sem)


def _drain(src, dst, sem, tokens, repeats):
    n = tokens * TOKEN_TILE
    for _ in range(repeats):
        pltpu.make_async_copy(src.at[pl.ds(0, n)], dst.at[pl.ds(0, n)], sem).wait()


def _dispatch_kernel(pads_ref, dest_ref, h_ref, xs_ref, zero_ref, sem, *, tokens, pads):
    def body(i, carry):
        for k in range(TOP_K):
            _row_copy(h_ref, xs_ref, i, dest_ref[k * tokens + i], sem).start(priority=k % 2)
        return carry

    lax.fori_loop(0, tokens, body, 0)
    zero_ref[...] = jnp.zeros(zero_ref.shape, zero_ref.dtype)
    pad0 = pl.program_id(0) * pads

    def fill(i, carry):
        _row_copy(zero_ref, xs_ref, 0, pads_ref[pad0 + i], sem).start()
        return carry

    lax.fori_loop(0, pads, fill, 0)
    _drain(h_ref, xs_ref, sem, tokens, TOP_K + pads // tokens)
    if pads % tokens:
        _drain(h_ref, xs_ref, sem, pads % tokens, 1)


def _dispatch(dest_flat, pad_rows, h, out_tokens, tokens):
    t = h.shape[0] // TOKEN_TILE
    steps = t // tokens
    pads = pad_rows.shape[0] // steps
    assert pads * steps == pad_rows.shape[0]
    return pl.pallas_call(
        functools.partial(_dispatch_kernel, tokens=tokens, pads=pads),
        grid_spec=pltpu.PrefetchScalarGridSpec(
            num_scalar_prefetch=1,
            grid=(steps,),
            in_specs=[
                pl.BlockSpec((tokens * TOP_K,), lambda i, pr: (i,), memory_space=pltpu.SMEM),
                pl.BlockSpec((tokens * TOKEN_TILE, LANES), lambda i, pr: (i, 0)),
            ],
            out_specs=pl.BlockSpec(memory_space=pl.ANY),
            scratch_shapes=[pltpu.VMEM((TOKEN_TILE, LANES), h.dtype), pltpu.SemaphoreType.DMA(())],
        ),
        out_shape=jax.ShapeDtypeStruct((out_tokens * TOKEN_TILE, LANES), h.dtype),
        compiler_params=_params("arbitrary"),
        name="moe_dispatch",
    )(pad_rows, dest_flat, h)


def _expert_kernel(be_ref, xs_ref, wg_ref, wu_ref, wd_ref, o_ref):
    xb = _rows_from_tiles(xs_ref, MOE_ROWS, TOKEN_TILE).astype(BF16)
    gate = jnp.dot(xb, wg_ref[0], preferred_element_type=F32)
    up = jnp.dot(xb, wu_ref[0], preferred_element_type=F32)
    hb = (gate * jax.nn.sigmoid(gate) * up).astype(BF16)
    _rows_to_tiles(o_ref, jnp.dot(hb, wd_ref[0], preferred_element_type=F32))


def _experts(xs, block_e, w_gate, w_up, w_down):
    d, f = w_gate.shape[1:]
    assert d == TOKEN_TILE * LANES
    nb = xs.shape[0] // (MOE_ROWS * TOKEN_TILE)
    rows = pl.BlockSpec((MOE_ROWS * TOKEN_TILE, LANES), lambda i, be: (i, 0))
    return pl.pallas_call(
        _expert_kernel,
        grid_spec=pltpu.PrefetchScalarGridSpec(
            num_scalar_prefetch=1,
            grid=(nb,),
            in_specs=[
                rows,
                pl.BlockSpec((1, d, f), lambda i, be: (be[i], 0, 0)),
                pl.BlockSpec((1, d, f), lambda i, be: (be[i], 0, 0)),
                pl.BlockSpec((1, f, d), lambda i, be: (be[i], 0, 0)),
            ],
            out_specs=rows,
        ),
        out_shape=jax.ShapeDtypeStruct(xs.shape, F32),
        compiler_params=_params("arbitrary"),
        name="moe_experts",
    )(block_e, xs, w_gate, w_up, w_down)


def _combine_kernel(dest_ref, x_ref, h_ref, eo_ref, w_ref, mod_ref, sg_ref, su_ref, sd_ref, fg_ref, o_ref,
                    buf_ref, sem, *, final_norm):
    tm = x_ref.shape[1]

    def gather(i, carry):
        for k in range(TOP_K):
            _row_copy(eo_ref, buf_ref.at[k], dest_ref[k * tm + i], i, sem).start(priority=k % 2)
        return carry

    lax.fori_loop(0, tm, gather, 0)
    hb = _rows_from_tiles(h_ref, tm, TOKEN_TILE).astype(BF16)
    gate = jnp.dot(hb, sg_ref[...], preferred_element_type=F32)
    up = jnp.dot(hb, su_ref[...], preferred_element_type=F32)
    y = jnp.dot((gate * jax.nn.sigmoid(gate) * up).astype(BF16), sd_ref[...], preferred_element_type=F32)
    w = w_ref[...]
    _drain(eo_ref, buf_ref.at[0], sem, tm, TOP_K)
    for k in range(TOP_K):
        y = y + _rows_from_tiles(buf_ref, tm, TOKEN_TILE, lead=(k,)) * w[:, k:k + 1]
    out = x_ref[0] + mod_ref[0][5:6] * y
    if final_norm:
        out = out * lax.rsqrt(jnp.mean(out * out, axis=-1, keepdims=True) + EPS) * fg_ref[...]
    o_ref[0] = out


def _combine(dest_flat, x, h, eo, wts, modv, ws_gate, ws_up, ws_down, final_g, final_norm, tm):
    b, s, d = x.shape
    f = ws_gate.shape[-1]
    nst = s // tm
    tok = pl.BlockSpec((1, tm, d), lambda bi, si: (bi, si, 0))
    tr = tm * TOKEN_TILE
    return pl.pallas_call(
        functools.partial(_combine_kernel, final_norm=final_norm),
        grid=(b, nst),
        in_specs=[
            pl.BlockSpec((tm * TOP_K,), lambda bi, si: (bi * nst + si,), memory_space=pltpu.SMEM),
            tok,
            pl.BlockSpec((tr, LANES), lambda bi, si: (bi * nst + si, 0)),
            pl.BlockSpec(memory_space=pl.ANY),
            pl.BlockSpec((tm, TOP_K), lambda bi, si: (bi * nst + si, 0)),
            pl.BlockSpec((1, N_MOD, d), lambda bi, si: (bi, 0, 0)),
            _const_spec((d, f)), _const_spec((d, f)), _const_spec((f, d)),
            _const_spec((1, d)),
        ],
        out_specs=tok,
        out_shape=jax.ShapeDtypeStruct((b, s, d), F32),
        scratch_shapes=[pltpu.VMEM((TOP_K, tr, LANES), F32), pltpu.SemaphoreType.DMA(())],
        compiler_params=_params("arbitrary", "arbitrary"),
        name="moe_combine",
    )(dest_flat, x, h, eo, wts, modv, ws_gate.astype(BF16), ws_up.astype(BF16), ws_down.astype(BF16),
      final_g.reshape(1, d))


def _moe_layer(x, modv, norm_g, w_router, router_bias, w_gate, w_up, w_down,
               ws_gate, ws_up, ws_down, final_g, final_norm):
    b, s, d = x.shape
    t = b * s
    nb = pl.cdiv(t * TOP_K, MOE_ROWS) + N_EXPERTS
    tile = _token_tile(s, 512)
    h, rank, eid, w6, cnt = _router(x, modv, norm_g, w_router, router_bias, tile)
    counts = cnt[:, 0].astype(jnp.int32)
    dest_flat, block_e, pad_rows = _dispatch_plan(rank[:TOP_K], eid[:TOP_K], counts, nb, tile)
    wts = w6[:TOP_K].T
    xs = _dispatch(dest_flat, pad_rows, h, nb * MOE_ROWS, tile)
    eo = _experts(xs, block_e, w_gate.astype(BF16), w_up.astype(BF16), w_down.astype(BF16))
    return _combine(dest_flat, x, h, eo, wts, modv, ws_gate, ws_up, ws_down, final_g, final_norm, tile)


def kernel(x, c, positions, norm1_g, norm2_g, ada_w, ada_b, da_w_in, da_lam_q1, da_lam_k1, da_lam_q2, da_lam_k2, da_subln_g, da_w_out, sg_w_in, sg_ln_g, sg_ln_b, sg_w_s, sg_b_s, sg_w_out, moe_w_router, moe_router_bias, moe_w_gate, moe_w_up, moe_w_down, moe_ws_gate, moe_ws_up, moe_ws_down, final_g):
    depth = ada_w.shape[0]
    mod = _ada_mod(c, ada_w, ada_b)
    for i in range(depth):
        j = i // 2
        if i % 2 == 0:
            lambda_init = 0.8 - 0.6 * math.exp(-0.3 * i)
            q, k, v = _qkv_rope(x, mod[i], norm1_g[i], positions, da_w_in[j].astype(BF16))
            lam_params = jnp.stack([da_lam_q1[j], da_lam_k1[j], da_lam_q2[j], da_lam_k2[j]])
            o = _diff_attention(q, k, v, lam_params, da_subln_g[j], lambda_init)
            x = _outproj_residual(o, x, mod[i], da_w_out[j].astype(BF16))
        else:
            x = _spatial_gating(x, mod[i], norm1_g[i], sg_w_in[j], sg_ln_g[j], sg_ln_b[j],
                                sg_w_s[j], sg_b_s[j], sg_w_out[j])
        x = _moe_layer(x, mod[i], norm2_g[i], moe_w_router[i], moe_router_bias[i],
                       moe_w_gate[i], moe_w_up[i], moe_w_down[i],
                       moe_ws_gate[i], moe_ws_up[i], moe_ws_down[i],
                       final_g, final_norm=(i == depth - 1))
    return x
```

```python
import functools
import math

import jax
import jax.numpy as jnp
from jax import lax
from jax.experimental import pallas as pl
from jax.experimental.pallas import tpu as pltpu

F32 = jnp.float32
BF16 = jnp.bfloat16
HIGHEST = lax.Precision.HIGHEST

EPS = 1e-6
LANES = 128
N_EXPERTS = 64
TOP_K = 6
N_EXPERT_GROUPS = 8
TOPK_GROUPS = 4
GROUP_SIZE = N_EXPERTS // N_EXPERT_GROUPS
ROUTED_SCALE = 2.5
DA_HEAD_DIM = 64
ROPE_THETA = 10000.0
SG_CHUNK = 128
SG_GROUPS = 8
N_MOD = 8

MOE_ROWS = 512
VMEM_LIMIT = 56 * 1024 * 1024


def _params(*sem):
    return pltpu.CompilerParams(dimension_semantics=sem, vmem_limit_bytes=VMEM_LIMIT)


def _const_spec(shape):
    n = len(shape)
    return pl.BlockSpec(shape, lambda *_: (0,) * n, pipeline_mode=pl.Buffered(1))


def _token_tile(s, target):
    t = min(s, target)
    assert s % t == 0
    return t


def _rows_to_tiles(ref, val):
    n, d = val.shape
    per = d // LANES
    for i in range(per):
        ref[pl.ds(i, n, stride=per), :] = val[:, i * LANES:(i + 1) * LANES]


def _rows_from_tiles(ref, n, per, lead=()):
    return jnp.concatenate([ref[lead + (pl.ds(i, n, stride=per), slice(None))] for i in range(per)],
                           axis=-1)


def _norm_mod(x, g, shift, scale):
    y = x * lax.rsqrt(jnp.mean(x * x, axis=-1, keepdims=True) + EPS) * g
    return y * (1.0 + scale) + shift


def _mod_kernel(c_ref, w_ref, b_ref, o_ref):
    c = c_ref[...]
    cond = c * jax.nn.sigmoid(c)
    o_ref[0] = jnp.dot(cond, w_ref[0], preferred_element_type=F32, precision=HIGHEST) + b_ref[0]


def _ada_mod(c, ada_w, ada_b):
    depth, d, d6 = ada_w.shape
    b = c.shape[0]
    nj = d6 // d
    out = pl.pallas_call(
        _mod_kernel,
        grid=(depth, nj),
        in_specs=[
            pl.BlockSpec((b, d), lambda i, j: (0, 0)),
            pl.BlockSpec((1, d, d), lambda i, j: (i, 0, j)),
            pl.BlockSpec((1, 1, d), lambda i, j: (i, 0, j)),
        ],
        out_specs=pl.BlockSpec((1, b, d), lambda i, j: (i, 0, j)),
        out_shape=jax.ShapeDtypeStruct((depth, b, d6), F32),
        compiler_params=_params("arbitrary", "arbitrary"),
        name="ada_mod",
    )(c, ada_w, ada_b.reshape(depth, 1, d6))
    mod = out.reshape(depth, b, nj, d)
    return jnp.pad(mod, ((0, 0), (0, 0), (0, N_MOD - nj), (0, 0)))


def _qkv_kernel(x_ref, mod_ref, g_ref, pos_ref, freq_ref, w_ref, q_ref, k_ref, v_ref):
    x = x_ref[0]
    mod = mod_ref[0]
    d = x.shape[-1]
    h = _norm_mod(x, g_ref[...], mod[0:1], mod[1:2])
    qkv = jnp.dot(h.astype(BF16), w_ref[...], preferred_element_type=F32)
    ang = pos_ref[0].astype(F32) * freq_ref[...]
    cos = jnp.cos(ang)
    sin = jnp.sin(ang)
    lane = lax.broadcasted_iota(jnp.int32, ang.shape, 1)
    first_half = (lane % DA_HEAD_DIM) < (DA_HEAD_DIM // 2)
    sin_signed = jnp.where(first_half, -sin, sin)
    half = DA_HEAD_DIM // 2

    def rope(blk):
        partner = jnp.where(first_half, pltpu.roll(blk, LANES - half, 1), pltpu.roll(blk, half, 1))
        return blk * cos + partner * sin_signed

    q_scale = DA_HEAD_DIM ** -0.5 * math.log2(math.e)
    for cb in range(d // LANES):
        lo = cb * LANES
        q_ref[0, :, lo:lo + LANES] = (rope(qkv[:, lo:lo + LANES]) * q_scale).astype(BF16)
        k_ref[0, :, lo:lo + LANES] = rope(qkv[:, d + lo:d + lo + LANES]).astype(BF16)
    v_ref[0] = qkv[:, 2 * d:].astype(BF16)


def _qkv_rope(x, modv, norm_g, positions, w_in_bf16):
    b, s, d = x.shape
    tm = _token_tile(s, 512)
    inv_freq = ROPE_THETA ** (-jnp.arange(0, DA_HEAD_DIM, 2, dtype=F32) / DA_HEAD_DIM)
    freq = jnp.tile(inv_freq, LANES // (DA_HEAD_DIM // 2)).reshape(1, LANES)
    tok = pl.BlockSpec((1, tm, d), lambda bi, si: (bi, si, 0))
    out = jax.ShapeDtypeStruct((b, s, d), BF16)
    return pl.pallas_call(
        _qkv_kernel,
        grid=(b, s // tm),
        in_specs=[
            tok,
            pl.BlockSpec((1, N_MOD, d), lambda bi, si: (bi, 0, 0)),
            _const_spec((1, d)),
            pl.BlockSpec((1, tm, 1), lambda bi, si: (bi, si, 0)),
            _const_spec((1, LANES)),
            _const_spec((d, 3 * d)),
        ],
        out_specs=[tok, tok, tok],
        out_shape=[out, out, out],
        compiler_params=_params("parallel", "parallel"),
        name="qkv_rope",
    )(x, modv, norm_g.reshape(1, d), positions.reshape(b, s, 1), freq, w_in_bf16)


ATTN_ROW_CHUNK = 64


ATTN_NORM_CHUNK = 1024
ATTN_SAFE_BOUND = 50.0


def _subhead_sq_norms(x):
    xf = x.astype(F32)
    sq = (xf * xf).astype(BF16)
    row = lax.broadcasted_iota(jnp.int32, (LANES, LANES), 0)
    first = (row < DA_HEAD_DIM).astype(BF16)
    second = (row >= DA_HEAD_DIM).astype(BF16)
    return (jnp.dot(sq, first, preferred_element_type=F32),
            jnp.dot(sq, second, preferred_element_type=F32))


def _attn_kernel(lam_ref, q_ref, k_ref, v_ref, g_ref, o_ref, qs_ref, s_ref, p_ref,
                 m_ref, l_ref, acc_ref, kmax_ref, *, lambda_init, tq, tk):
    rows = 2 * tq
    n_tiles = k_ref.shape[1] // tk

    @pl.when(pl.program_id(2) == 0)
    def _():
        chunk = min(ATTN_NORM_CHUNK, k_ref.shape[1])

        def body(c, carry):
            off = pl.multiple_of(c * chunk, chunk)
            n1, n2 = _subhead_sq_norms(k_ref[0, pl.ds(off, chunk), :])
            return (jnp.maximum(carry[0], jnp.max(n1, axis=0, keepdims=True)),
                    jnp.maximum(carry[1], jnp.max(n2, axis=0, keepdims=True)))

        zeros = jnp.zeros((1, LANES), F32)
        k1, k2 = lax.fori_loop(0, k_ref.shape[1] // chunk, body, (zeros, zeros))
        kmax_ref[0:1] = k1
        kmax_ref[1:2] = k2

    q = q_ref[0]
    lane = lax.broadcasted_iota(jnp.int32, q.shape, 1)
    zero = jnp.zeros_like(q)
    qs_ref[0:tq] = jnp.where(lane < DA_HEAD_DIM, q, zero)
    qs_ref[tq:rows] = jnp.where(lane >= DA_HEAD_DIM, q, zero)
    l_ref[...] = jnp.zeros(l_ref.shape, F32)
    acc_ref[...] = jnp.zeros(acc_ref.shape, F32)
    qn1, qn2 = _subhead_sq_norms(q)
    bound1 = jnp.sqrt(qn1 * kmax_ref[0:1])
    bound2 = jnp.sqrt(qn2 * kmax_ref[1:2])
    fixed_shift = jnp.maximum(jnp.max(bound1), jnp.max(bound2)) <= ATTN_SAFE_BOUND

    def scores(tile):
        off = pl.multiple_of(tile * tk, tk)
        return lax.dot_general(qs_ref[...], k_ref[0, pl.ds(off, tk), :], (((1,), (1,)), ((), ())),
                               preferred_element_type=F32)

    def add_pv(tile):
        off = pl.multiple_of(tile * tk, tk)
        acc_ref[...] += jnp.dot(p_ref[...], v_ref[0, pl.ds(off, tk), :], preferred_element_type=F32)

    @pl.when(fixed_shift)
    def _():
        m_ref[0:tq] = bound1
        m_ref[tq:rows] = bound2

        def step(tile, carry):
            s = scores(tile)
            for r0 in range(0, rows, ATTN_ROW_CHUNK):
                rs = slice(r0, r0 + ATTN_ROW_CHUNK)
                m = m_ref[rs]
                psum = None
                for c0 in range(0, tk, LANES):
                    p = jnp.exp2(s[rs, c0:c0 + LANES] - m)
                    psum = p if psum is None else psum + p
                    p_ref[rs, c0:c0 + LANES] = p.astype(BF16)
                l_ref[rs] += psum
            add_pv(tile)
            return carry

        lax.fori_loop(0, n_tiles, step, 0)

    @pl.when(jnp.logical_not(fixed_shift))
    def _():
        m_ref[...] = jnp.full(m_ref.shape, -jnp.inf, F32)

        def step(tile, carry):
            s_ref[...] = scores(tile)
            for r0 in range(0, rows, ATTN_ROW_CHUNK):
                rs = slice(r0, r0 + ATTN_ROW_CHUNK)
                mx = s_ref[rs, 0:LANES]
                for c0 in range(LANES, tk, LANES):
                    mx = jnp.maximum(mx, s_ref[rs, c0:c0 + LANES])
                m_prev = m_ref[rs]
                m_new = jnp.maximum(m_prev, jnp.max(mx, axis=-1, keepdims=True))
                alpha = jnp.exp2(m_prev - m_new)
                psum = None
                for c0 in range(0, tk, LANES):
                    p = jnp.exp2(s_ref[rs, c0:c0 + LANES] - m_new)
                    psum = p if psum is None else psum + p
                    p_ref[rs, c0:c0 + LANES] = p.astype(BF16)
                l_ref[rs] = alpha * l_ref[rs] + psum
                m_ref[rs] = m_new
                acc_ref[rs] = alpha * acc_ref[rs]
            add_pv(tile)
            return carry

        lax.fori_loop(0, n_tiles, step, 0)

    o = acc_ref[...] / jnp.sum(l_ref[...], axis=-1, keepdims=True)
    lp = lam_ref[...]
    lam = (jnp.exp(jnp.sum(lp[0:1] * lp[1:2], axis=-1, keepdims=True))
           - jnp.exp(jnp.sum(lp[2:3] * lp[3:4], axis=-1, keepdims=True)) + lambda_init)
    diff = o[0:tq] - lam * o[tq:2 * tq]
    y = diff * lax.rsqrt(jnp.mean(diff * diff, axis=-1, keepdims=True) + EPS) * g_ref[...]
    o_ref[0] = (y * (1.0 - lambda_init)).astype(BF16)


def _diff_attention(q, k, v, lam_params, subln_g, lambda_init):
    b, s, d = q.shape
    hw = 2 * DA_HEAD_DIM
    nh = d // hw
    tq = _token_tile(s, 1024)
    tk = _token_tile(s, 2048)
    kern = functools.partial(_attn_kernel, lambda_init=lambda_init, tq=tq, tk=tk)
    return pl.pallas_call(
        kern,
        grid=(b, nh, s // tq),
        in_specs=[
            _const_spec((4, DA_HEAD_DIM)),
            pl.BlockSpec((1, tq, hw), lambda bi, hi, qi: (bi, qi, hi)),
            pl.BlockSpec((1, s, hw), lambda bi, hi, qi: (bi, 0, hi)),
            pl.BlockSpec((1, s, hw), lambda bi, hi, qi: (bi, 0, hi)),
            _const_spec((1, hw)),
        ],
        out_specs=pl.BlockSpec((1, tq, hw), lambda bi, hi, qi: (bi, qi, hi)),
        out_shape=jax.ShapeDtypeStruct((b, s, d), BF16),
        scratch_shapes=[
            pltpu.VMEM((2 * tq, hw), BF16),
            pltpu.VMEM((2 * tq, tk), F32),
            pltpu.VMEM((2 * tq, tk), BF16),
            pltpu.VMEM((2 * tq, LANES), F32),
            pltpu.VMEM((2 * tq, LANES), F32),
            pltpu.VMEM((2 * tq, hw), F32),
            pltpu.VMEM((8, LANES), F32),
        ],
        compiler_params=_params("parallel", "parallel", "arbitrary"),
        name="diff_attn",
    )(lam_params, q, k, v, subln_g.reshape(1, hw))


def _outproj_kernel(o_ref, x_ref, mod_ref, w_ref, x1_ref):
    mix = jnp.dot(o_ref[0], w_ref[...], preferred_element_type=F32)
    x1_ref[0] = x_ref[0] + mod_ref[0][2:3] * mix


def _outproj_residual(o, x, modv, w_out_bf16):
    b, s, d = x.shape
    tm = _token_tile(s, 512)
    tok = pl.BlockSpec((1, tm, d), lambda bi, si: (bi, si, 0))
    return pl.pallas_call(
        _outproj_kernel,
        grid=(b, s // tm),
        in_specs=[tok, tok, pl.BlockSpec((1, N_MOD, d), lambda bi, si: (bi, 0, 0)), _const_spec((d, d))],
        out_specs=tok,
        out_shape=jax.ShapeDtypeStruct((b, s, d), F32),
        compiler_params=_params("parallel", "parallel"),
        name="attn_outproj",
    )(o, x, modv, w_out_bf16)


def _sg_kernel(x_ref, mod_ref, g_ref, win_ref, lng_ref, lnb_ref, ws_ref, bs_ref, wout_ref,
               x1_ref, gated_ref):
    x = x_ref[0]
    mod = mod_ref[0]
    tm = x.shape[0]
    half = lng_ref.shape[-1]
    gdim = half // SG_GROUPS
    h = _norm_mod(x, g_ref[...], mod[0:1], mod[1:2])
    z = jnp.dot(h.astype(BF16), win_ref[...], preferred_element_type=F32)
    z = 0.5 * z * (1.0 + lax.erf(z * (2.0 ** -0.5)))
    u = z[:, :half]
    v = z[:, half:]
    mu = jnp.mean(v, axis=-1, keepdims=True)
    vc = v - mu
    v = vc * lax.rsqrt(jnp.mean(vc * vc, axis=-1, keepdims=True) + EPS) * lng_ref[...] + lnb_ref[...]
    vb = v.astype(BF16)
    for c in range(tm // SG_CHUNK):
        r0 = c * SG_CHUNK
        for g in range(SG_GROUPS):
            c0 = g * gdim
            sp = jnp.dot(ws_ref[g], vb[r0:r0 + SG_CHUNK, c0:c0 + gdim],
                         preferred_element_type=F32) + bs_ref[g]
            gated_ref[r0:r0 + SG_CHUNK, c0:c0 + gdim] = (
                u[r0:r0 + SG_CHUNK, c0:c0 + gdim] * sp).astype(BF16)
    mix = jnp.dot(gated_ref[...], wout_ref[...], preferred_element_type=F32)
    x1_ref[0] = x + mod[2:3] * mix


def _spatial_gating(x, modv, norm_g, w_in, ln_g, ln_b, w_s, b_s, w_out):
    b, s, d = x.shape
    half = ln_g.shape[-1]
    tm = _token_tile(s, 512)
    assert tm % SG_CHUNK == 0
    tok = pl.BlockSpec((1, tm, d), lambda bi, si: (bi, si, 0))
    return pl.pallas_call(
        _sg_kernel,
        grid=(b, s // tm),
        in_specs=[
            tok,
            pl.BlockSpec((1, N_MOD, d), lambda bi, si: (bi, 0, 0)),
            _const_spec((1, d)),
            _const_spec((d, 2 * half)),
            _const_spec((1, half)),
            _const_spec((1, half)),
            _const_spec((SG_GROUPS, SG_CHUNK, SG_CHUNK)),
            _const_spec((SG_GROUPS, SG_CHUNK, 1)),
            _const_spec((half, d)),
        ],
        out_specs=tok,
        out_shape=jax.ShapeDtypeStruct((b, s, d), F32),
        scratch_shapes=[pltpu.VMEM((tm, half), BF16)],
        compiler_params=_params("parallel", "parallel"),
        name="spatial_gating",
    )(x, modv, norm_g.reshape(1, d), w_in.astype(BF16), ln_g.reshape(1, half), ln_b.reshape(1, half),
      w_s.astype(BF16), b_s.reshape(SG_GROUPS, SG_CHUNK, 1), w_out.astype(BF16))


def _first_argmax(cur, idx, sentinel):
    m = jnp.max(cur, axis=0, keepdims=True)
    first = jnp.min(jnp.where(cur == m, idx, sentinel), axis=0, keepdims=True)
    return m, idx == first


def _router_kernel(x_ref, mod_ref, g_ref, wrt_ref, bias_ref, upper_ref, lower_ref,
                   h_ref, rank_ref, eid_ref, w_ref, cnt_ref):
    mod = mod_ref[0]
    h = _norm_mod(x_ref[0], g_ref[...], mod[3:4], mod[4:5])
    _rows_to_tiles(h_ref, h)
    logits = lax.dot_general(wrt_ref[...], h, (((1,), (1,)), ((), ())),
                             precision=HIGHEST, preferred_element_type=F32)
    scores = jax.nn.sigmoid(logits)
    sel = scores + bias_ref[...]
    tm = sel.shape[1]
    neg = -jnp.inf
    sub = lax.broadcasted_iota(jnp.int32, (GROUP_SIZE, tm), 0)
    rows = []
    for g in range(N_EXPERT_GROUPS):
        blk = sel[g * GROUP_SIZE:(g + 1) * GROUP_SIZE]
        m1, hit = _first_argmax(blk, sub, GROUP_SIZE)
        m2 = jnp.max(jnp.where(hit, neg, blk), axis=0, keepdims=True)
        rows.append(m1 + m2)
    cur = jnp.concatenate(rows, axis=0)
    gsel = jnp.zeros(cur.shape, jnp.int32)
    for _ in range(TOPK_GROUPS):
        _, hit = _first_argmax(cur, sub, N_EXPERT_GROUPS)
        gsel = jnp.where(hit, 1, gsel)
        cur = jnp.where(hit, neg, cur)
    masked = []
    for g in range(N_EXPERT_GROUPS):
        blk = sel[g * GROUP_SIZE:(g + 1) * GROUP_SIZE]
        masked.append(jnp.where(gsel[g:g + 1] > 0, blk, neg))
    cur = jnp.concatenate(masked, axis=0)
    eidx = lax.broadcasted_iota(jnp.int32, cur.shape, 0)
    chosen = jnp.zeros(cur.shape, jnp.int32)
    for _ in range(TOP_K):
        _, hit = _first_argmax(cur, eidx, N_EXPERTS)
        chosen = jnp.where(hit, 1, chosen)
        cur = jnp.where(hit, neg, cur)
    picked = jnp.where(chosen > 0, scores, 0.0)
    wsum = jnp.sum(picked, axis=0, keepdims=True)
    wdense = picked / wsum * ROUTED_SCALE

    @pl.when((pl.program_id(0) == 0) & (pl.program_id(1) == 0))
    def _():
        cnt_ref[...] = jnp.zeros(cnt_ref.shape, F32)

    cm = chosen.astype(BF16)
    base = cnt_ref[...]
    rank = (jnp.dot(cm, upper_ref[...], preferred_element_type=F32)
            + jnp.concatenate([base] * (tm // LANES), axis=1))
    cnt_ref[...] = base + jnp.dot(cm, jnp.ones((tm, LANES), BF16), preferred_element_type=F32)
    ordinal = jnp.dot(lower_ref[...], cm, preferred_element_type=F32)
    eid = eidx.astype(F32)
    rows_r, rows_e, rows_w = [], [], []
    for k in range(TOP_K):
        pick = (chosen > 0) & (ordinal == k)
        rows_r.append(jnp.sum(jnp.where(pick, rank, 0.0), axis=0, keepdims=True))
        rows_e.append(jnp.sum(jnp.where(pick, eid, 0.0), axis=0, keepdims=True))
        rows_w.append(jnp.sum(jnp.where(pick, wdense, 0.0), axis=0, keepdims=True))
    fill = [jnp.zeros((N_MOD - TOP_K, tm), F32)]
    rank_ref[...] = jnp.concatenate(rows_r + fill, axis=0).astype(jnp.int32)
    eid_ref[...] = jnp.concatenate(rows_e + fill, axis=0).astype(jnp.int32)
    w_ref[...] = jnp.concatenate(rows_w + fill, axis=0)


def _router(x, modv, norm_g, w_router, router_bias, tm):
    b, s, d = x.shape
    t = b * s
    e = w_router.shape[1]
    nst = s // tm
    tok = pl.BlockSpec((1, tm, d), lambda bi, si: (bi, si, 0))
    per_k = pl.BlockSpec((N_MOD, tm), lambda bi, si: (0, bi * nst + si))
    upper = (jnp.arange(tm)[:, None] < jnp.arange(tm)[None, :]).astype(BF16)
    lower = (jnp.arange(e)[None, :] < jnp.arange(e)[:, None]).astype(BF16)
    return pl.pallas_call(
        _router_kernel,
        grid=(b, nst),
        in_specs=[
            tok,
            pl.BlockSpec((1, N_MOD, d), lambda bi, si: (bi, 0, 0)),
            _const_spec((1, d)),
            _const_spec((e, d)),
            _const_spec((e, 1)),
            _const_spec((tm, tm)),
            _const_spec((e, e)),
        ],
        out_specs=[pl.BlockSpec((tm * (d // LANES), LANES), lambda bi, si: (bi * nst + si, 0)),
                   per_k, per_k, per_k,
                   pl.BlockSpec((e, LANES), lambda bi, si: (0, 0))],
        out_shape=[jax.ShapeDtypeStruct((t * (d // LANES), LANES), F32),
                   jax.ShapeDtypeStruct((N_MOD, t), jnp.int32),
                   jax.ShapeDtypeStruct((N_MOD, t), jnp.int32),
                   jax.ShapeDtypeStruct((N_MOD, t), F32),
                   jax.ShapeDtypeStruct((e, LANES), F32)],
        compiler_params=_params("arbitrary", "arbitrary"),
        name="moe_router",
    )(x, modv, norm_g.reshape(1, d), w_router.T, router_bias.reshape(e, 1), upper, lower)


def _dispatch_plan(rank, eid, counts, n_blocks, tile):
    e = counts.shape[0]
    t = rank.shape[1]
    padded = (counts + MOE_ROWS - 1) // MOE_ROWS * MOE_ROWS
    pad_end = jnp.cumsum(padded)
    pad_start = pad_end - padded
    start_of = jnp.sum(jnp.where(eid[:, :, None] == jnp.arange(e)[None, None, :], pad_start[None, None, :], 0),
                       axis=-1)
    dest = (rank + start_of).astype(jnp.int32)
    dest = dest.reshape(TOP_K, t // tile, tile).transpose(1, 0, 2).reshape(t * TOP_K)
    blk_start = jnp.arange(n_blocks, dtype=jnp.int32) * MOE_ROWS
    block_e = jnp.minimum(jnp.sum(pad_end[None, :] <= blk_start[:, None], axis=1), e - 1).astype(jnp.int32)
    slack = padded - counts
    slack_end = jnp.cumsum(slack)
    q = jnp.arange(n_blocks * MOE_ROWS - t * TOP_K, dtype=jnp.int32)
    eq = jnp.sum(slack_end[None, :] <= q[:, None], axis=1)
    ec = jnp.minimum(eq, e - 1)
    in_expert = pad_start[ec] + counts[ec] + q - (slack_end[ec] - slack[ec])
    pad_rows = jnp.where(eq < e, in_expert, pad_end[-1] + q - slack_end[-1]).astype(jnp.int32)
    return dest, block_e, pad_rows


TOKEN_TILE = 8


def _row_copy(src, dst, s_tok, d_tok, sem):
    s0 = pl.multiple_of(s_tok * TOKEN_TILE, TOKEN_TILE)
    d0 = pl.multiple_of(d_tok * TOKEN_TILE, TOKEN_TILE)
    return pltpu.make_async_copy(src.at[pl.ds(s0, TOKEN_TILE)], dst.at[pl.ds(d0, TOKEN_TILE)], sem)


def _drain(src, dst, sem, tokens, repeats):
    n = tokens * TOKEN_TILE
    for _ in range(repeats):
        pltpu.make_async_copy(src.at[pl.ds(0, n)], dst.at[pl.ds(0, n)], sem).wait()


def _dispatch_kernel(pads_ref, dest_ref, h_ref, xs_ref, zero_ref, sem, *, tokens, pads):
    def body(i, carry):
        for k in range(TOP_K):
            _row_copy(h_ref, xs_ref, i, dest_ref[k * tokens + i], sem).start(priority=k % 2)
        return carry

    lax.fori_loop(0, tokens, body, 0)
    zero_ref[...] = jnp.zeros(zero_ref.shape, zero_ref.dtype)
    pad0 = pl.program_id(0) * pads

    def fill(i, carry):
        _row_copy(zero_ref, xs_ref, 0, pads_ref[pad0 + i], sem).start()
        return carry

    lax.fori_loop(0, pads, fill, 0)
    _drain(h_ref, xs_ref, sem, tokens, TOP_K + pads // tokens)
    if pads % tokens:
        _drain(h_ref, xs_ref, sem, pads % tokens, 1)


def _dispatch(dest_flat, pad_rows, h, out_tokens, tokens):
    t = h.shape[0] // TOKEN_TILE
    steps = t // tokens
    pads = pad_rows.shape[0] // steps
    assert pads * steps == pad_rows.shape[0]
    return pl.pallas_call(
        functools.partial(_dispatch_kernel, tokens=tokens, pads=pads),
        grid_spec=pltpu.PrefetchScalarGridSpec(
            num_scalar_prefetch=1,
            grid=(steps,),
            in_specs=[
                pl.BlockSpec((tokens * TOP_K,), lambda i, pr: (i,), memory_space=pltpu.SMEM),
                pl.BlockSpec((tokens * TOKEN_TILE, LANES), lambda i, pr: (i, 0)),
            ],
            out_specs=pl.BlockSpec(memory_space=pl.ANY),
            scratch_shapes=[pltpu.VMEM((TOKEN_TILE, LANES), h.dtype), pltpu.SemaphoreType.DMA(())],
        ),
        out_shape=jax.ShapeDtypeStruct((out_tokens * TOKEN_TILE, LANES), h.dtype),
        compiler_params=_params("arbitrary"),
        name="moe_dispatch",
    )(pad_rows, dest_flat, h)


def _expert_kernel(be_ref, xs_ref, wg_ref, wu_ref, wd_ref, o_ref, wg_s, wu_s, wd_s):
    i = pl.program_id(0)

    @pl.when((i == 0) | (be_ref[i] != be_ref[jnp.maximum(i - 1, 0)]))
    def _():
        wg_s[...] = wg_ref[0].astype(BF16)
        wu_s[...] = wu_ref[0].astype(BF16)
        wd_s[...] = wd_ref[0].astype(BF16)

    xb = _rows_from_tiles(xs_ref, MOE_ROWS, TOKEN_TILE).astype(BF16)
    gate = jnp.dot(xb, wg_s[...], preferred_element_type=F32)
    up = jnp.dot(xb, wu_s[...], preferred_element_type=F32)
    hb = (gate * jax.nn.sigmoid(gate) * up).astype(BF16)
    _rows_to_tiles(o_ref, jnp.dot(hb, wd_s[...], preferred_element_type=F32))


def _experts(xs, block_e, w_gate, w_up, w_down, layer):
    d, f = w_gate.shape[2:]
    assert d == TOKEN_TILE * LANES
    nb = xs.shape[0] // (MOE_ROWS * TOKEN_TILE)
    rows = pl.BlockSpec((MOE_ROWS * TOKEN_TILE, LANES), lambda i, be: (i, 0))
    return pl.pallas_call(
        _expert_kernel,
        grid_spec=pltpu.PrefetchScalarGridSpec(
            num_scalar_prefetch=1,
            grid=(nb,),
            in_specs=[
                rows,
                pl.BlockSpec((None, 1, d, f), lambda i, be: (layer, be[i], 0, 0)),
                pl.BlockSpec((None, 1, d, f), lambda i, be: (layer, be[i], 0, 0)),
                pl.BlockSpec((None, 1, f, d), lambda i, be: (layer, be[i], 0, 0)),
            ],
            out_specs=rows,
            scratch_shapes=[pltpu.VMEM((d, f), BF16), pltpu.VMEM((d, f), BF16), pltpu.VMEM((f, d), BF16)],
        ),
        out_shape=jax.ShapeDtypeStruct(xs.shape, F32),
        compiler_params=_params("arbitrary"),
        name="moe_experts",
    )(block_e, xs, w_gate, w_up, w_down)


def _combine_kernel(dest_ref, x_ref, h_ref, eo_ref, w_ref, mod_ref, sg_ref, su_ref, sd_ref, fg_ref, o_ref,
                    buf_ref, sem, *, final_norm):
    tm = x_ref.shape[1]

    def gather(i, carry):
        for k in range(TOP_K):
            _row_copy(eo_ref, buf_ref.at[k], dest_ref[k * tm + i], i, sem).start(priority=k % 2)
        return carry

    lax.fori_loop(0, tm, gather, 0)
    hb = _rows_from_tiles(h_ref, tm, TOKEN_TILE).astype(BF16)
    gate = jnp.dot(hb, sg_ref[...], preferred_element_type=F32)
    up = jnp.dot(hb, su_ref[...], preferred_element_type=F32)
    y = jnp.dot((gate * jax.nn.sigmoid(gate) * up).astype(BF16), sd_ref[...], preferred_element_type=F32)
    w = w_ref[...]
    _drain(eo_ref, buf_ref.at[0], sem, tm, TOP_K)
    for k in range(TOP_K):
        y = y + _rows_from_tiles(buf_ref, tm, TOKEN_TILE, lead=(k,)) * w[:, k:k + 1]
    out = x_ref[0] + mod_ref[0][5:6] * y
    if final_norm:
        out = out * lax.rsqrt(jnp.mean(out * out, axis=-1, keepdims=True) + EPS) * fg_ref[...]
    o_ref[0] = out


def _combine(dest_flat, x, h, eo, wts, modv, ws_gate, ws_up, ws_down, final_g, final_norm, tm):
    b, s, d = x.shape
    f = ws_gate.shape[-1]
    nst = s // tm
    tok = pl.BlockSpec((1, tm, d), lambda bi, si: (bi, si, 0))
    tr = tm * TOKEN_TILE
    return pl.pallas_call(
        functools.partial(_combine_kernel, final_norm=final_norm),
        grid=(b, nst),
        in_specs=[
            pl.BlockSpec((tm * TOP_K,), lambda bi, si: (bi * nst + si,), memory_space=pltpu.SMEM),
            tok,
            pl.BlockSpec((tr, LANES), lambda bi, si: (bi * nst + si, 0)),
            pl.BlockSpec(memory_space=pl.ANY),
            pl.BlockSpec((tm, TOP_K), lambda bi, si: (bi * nst + si, 0)),
            pl.BlockSpec((1, N_MOD, d), lambda bi, si: (bi, 0, 0)),
            _const_spec((d, f)), _const_spec((d, f)), _const_spec((f, d)),
            _const_spec((1, d)),
        ],
        out_specs=tok,
        out_shape=jax.ShapeDtypeStruct((b, s, d), F32),
        scratch_shapes=[pltpu.VMEM((TOP_K, tr, LANES), F32), pltpu.SemaphoreType.DMA(())],
        compiler_params=_params("arbitrary", "arbitrary"),
        name="moe_combine",
    )(dest_flat, x, h, eo, wts, modv, ws_gate.astype(BF16), ws_up.astype(BF16), ws_down.astype(BF16),
      final_g.reshape(1, d))


def _moe_layer(x, modv, norm_g, w_router, router_bias, w_gate, w_up, w_down, layer,
               ws_gate, ws_up, ws_down, final_g, final_norm):
    b, s, d = x.shape
    t = b * s
    nb = pl.cdiv(t * TOP_K, MOE_ROWS) + N_EXPERTS
    tile = _token_tile(s, 512)
    h, rank, eid, w6, cnt = _router(x, modv, norm_g, w_router, router_bias, tile)
    counts = cnt[:, 0].astype(jnp.int32)
    dest_flat, block_e, pad_rows = _dispatch_plan(rank[:TOP_K], eid[:TOP_K], counts, nb, tile)
    wts = w6[:TOP_K].T
    xs = _dispatch(dest_flat, pad_rows, h, nb * MOE_ROWS, tile)
    eo = _experts(xs, block_e, w_gate, w_up, w_down, layer)
    return _combine(dest_flat, x, h, eo, wts, modv, ws_gate, ws_up, ws_down, final_g, final_norm, tile)


def kernel(x, c, positions, norm1_g, norm2_g, ada_w, ada_b, da_w_in, da_lam_q1, da_lam_k1, da_lam_q2, da_lam_k2, da_subln_g, da_w_out, sg_w_in, sg_ln_g, sg_ln_b, sg_w_s, sg_b_s, sg_w_out, moe_w_router, moe_router_bias, moe_w_gate, moe_w_up, moe_w_down, moe_ws_gate, moe_ws_up, moe_ws_down, final_g):
    depth = ada_w.shape[0]
    mod = _ada_mod(c, ada_w, ada_b)
    for i in range(depth):
        j = i // 2
        if i % 2 == 0:
            lambda_init = 0.8 - 0.6 * math.exp(-0.3 * i)
            q, k, v = _qkv_rope(x, mod[i], norm1_g[i], positions, da_w_in[j].astype(BF16))
            lam_params = jnp.stack([da_lam_q1[j], da_lam_k1[j], da_lam_q2[j], da_lam_k2[j]])
            o = _diff_attention(q, k, v, lam_params, da_subln_g[j], lambda_init)
            x = _outproj_residual(o, x, mod[i], da_w_out[j].astype(BF16))
        else:
            x = _spatial_gating(x, mod[i], norm1_g[i], sg_w_in[j], sg_ln_g[j], sg_ln_b[j],
                                sg_w_s[j], sg_b_s[j], sg_w_out[j])
        x = _moe_layer(x, mod[i], norm2_g[i], moe_w_router[i], moe_router_bias[i],
                       moe_w_gate, moe_w_up, moe_w_down, i,
                       moe_ws_gate[i], moe_ws_up[i], moe_ws_down[i],
                       final_g, final_norm=(i == depth - 1))
    return x
```

```python
import functools
import math

import jax
import jax.numpy as jnp
from jax import lax
from jax.experimental import pallas as pl
from jax.experimental.pallas import tpu as pltpu

F32 = jnp.float32
BF16 = jnp.bfloat16
HIGHEST = lax.Precision.HIGHEST

EPS = 1e-6
LANES = 128
N_EXPERTS = 64
TOP_K = 6
N_EXPERT_GROUPS = 8
TOPK_GROUPS = 4
GROUP_SIZE = N_EXPERTS // N_EXPERT_GROUPS
ROUTED_SCALE = 2.5
DA_HEAD_DIM = 64
ROPE_THETA = 10000.0
SG_CHUNK = 128
SG_GROUPS = 8
N_MOD = 8

MOE_ROWS = 512
TOKEN_W = 128
TOKEN_TILE = 8
VMEM_LIMIT = 56 * 1024 * 1024


def _params(*sem):
    return pltpu.CompilerParams(dimension_semantics=sem, vmem_limit_bytes=VMEM_LIMIT)


def _const_spec(shape):
    n = len(shape)
    return pl.BlockSpec(shape, lambda *_: (0,) * n, pipeline_mode=pl.Buffered(1))


def _token_tile(s, target):
    t = min(s, target)
    assert s % t == 0
    return t


def _rows_to_tiles(ref, val):
    n, d = val.shape
    per = d // TOKEN_W
    for i in range(per):
        ref[pl.ds(i, n, stride=per), :] = val[:, i * TOKEN_W:(i + 1) * TOKEN_W]


def _rows_from_tiles(ref, n, per, lead=()):
    return jnp.concatenate([ref[lead + (pl.ds(i, n, stride=per), slice(None))] for i in range(per)],
                           axis=-1)


def _norm_mod(x, g, shift, scale):
    y = x * lax.rsqrt(jnp.mean(x * x, axis=-1, keepdims=True) + EPS) * g
    return y * (1.0 + scale) + shift


def _mod_kernel(c_ref, w_ref, b_ref, o_ref):
    c = c_ref[...]
    cond = c * jax.nn.sigmoid(c)
    o_ref[0] = jnp.dot(cond, w_ref[0], preferred_element_type=F32, precision=HIGHEST) + b_ref[0]


def _ada_mod(c, ada_w, ada_b):
    depth, d, d6 = ada_w.shape
    b = c.shape[0]
    nj = d6 // d
    out = pl.pallas_call(
        _mod_kernel,
        grid=(depth, nj),
        in_specs=[
            pl.BlockSpec((b, d), lambda i, j: (0, 0)),
            pl.BlockSpec((1, d, d), lambda i, j: (i, 0, j)),
            pl.BlockSpec((1, 1, d), lambda i, j: (i, 0, j)),
        ],
        out_specs=pl.BlockSpec((1, b, d), lambda i, j: (i, 0, j)),
        out_shape=jax.ShapeDtypeStruct((depth, b, d6), F32),
        compiler_params=_params("arbitrary", "arbitrary"),
        name="ada_mod",
    )(c, ada_w, ada_b.reshape(depth, 1, d6))
    mod = out.reshape(depth, b, nj, d)
    return jnp.pad(mod, ((0, 0), (0, 0), (0, N_MOD - nj), (0, 0)))


def _qkv_kernel(x_ref, mod_ref, g_ref, pos_ref, freq_ref, w_ref, q_ref, k_ref, v_ref):
    x = x_ref[0]
    mod = mod_ref[0]
    d = x.shape[-1]
    h = _norm_mod(x, g_ref[...], mod[0:1], mod[1:2])
    qkv = jnp.dot(h.astype(BF16), w_ref[...], preferred_element_type=F32)
    ang = pos_ref[0].astype(F32) * freq_ref[...]
    cos = jnp.cos(ang)
    sin = jnp.sin(ang)
    lane = lax.broadcasted_iota(jnp.int32, ang.shape, 1)
    first_half = (lane % DA_HEAD_DIM) < (DA_HEAD_DIM // 2)
    sin_signed = jnp.where(first_half, -sin, sin)
    half = DA_HEAD_DIM // 2

    def rope(blk):
        partner = jnp.where(first_half, pltpu.roll(blk, LANES - half, 1), pltpu.roll(blk, half, 1))
        return blk * cos + partner * sin_signed

    q_scale = DA_HEAD_DIM ** -0.5 * math.log2(math.e)
    for cb in range(d // LANES):
        lo = cb * LANES
        q_ref[0, :, lo:lo + LANES] = (rope(qkv[:, lo:lo + LANES]) * q_scale).astype(BF16)
        k_ref[0, :, lo:lo + LANES] = rope(qkv[:, d + lo:d + lo + LANES]).astype(BF16)
    v_ref[0] = qkv[:, 2 * d:].astype(BF16)


def _qkv_rope(x, modv, norm_g, positions, w_in_bf16):
    b, s, d = x.shape
    tm = _token_tile(s, 512)
    inv_freq = ROPE_THETA ** (-jnp.arange(0, DA_HEAD_DIM, 2, dtype=F32) / DA_HEAD_DIM)
    freq = jnp.tile(inv_freq, LANES // (DA_HEAD_DIM // 2)).reshape(1, LANES)
    tok = pl.BlockSpec((1, tm, d), lambda bi, si: (bi, si, 0))
    out = jax.ShapeDtypeStruct((b, s, d), BF16)
    return pl.pallas_call(
        _qkv_kernel,
        grid=(b, s // tm),
        in_specs=[
            tok,
            pl.BlockSpec((1, N_MOD, d), lambda bi, si: (bi, 0, 0)),
            _const_spec((1, d)),
            pl.BlockSpec((1, tm, 1), lambda bi, si: (bi, si, 0)),
            _const_spec((1, LANES)),
            _const_spec((d, 3 * d)),
        ],
        out_specs=[tok, tok, tok],
        out_shape=[out, out, out],
        compiler_params=_params("parallel", "parallel"),
        name="qkv_rope",
    )(x, modv, norm_g.reshape(1, d), positions.reshape(b, s, 1), freq, w_in_bf16)


ATTN_ROW_CHUNK = 64


ATTN_NORM_CHUNK = 1024
ATTN_SAFE_BOUND = 50.0


def _subhead_sq_norms(x):
    xf = x.astype(F32)
    sq = (xf * xf).astype(BF16)
    row = lax.broadcasted_iota(jnp.int32, (LANES, LANES), 0)
    first = (row < DA_HEAD_DIM).astype(BF16)
    second = (row >= DA_HEAD_DIM).astype(BF16)
    return (jnp.dot(sq, first, preferred_element_type=F32),
            jnp.dot(sq, second, preferred_element_type=F32))


def _attn_kernel(lam_ref, q_ref, k_ref, v_ref, g_ref, o_ref, qs_ref, s_ref, p_ref,
                 m_ref, l_ref, acc_ref, kmax_ref, *, lambda_init, tq, tk):
    rows = 2 * tq
    n_tiles = k_ref.shape[1] // tk

    @pl.when(pl.program_id(2) == 0)
    def _():
        chunk = min(ATTN_NORM_CHUNK, k_ref.shape[1])

        def body(c, carry):
            off = pl.multiple_of(c * chunk, chunk)
            n1, n2 = _subhead_sq_norms(k_ref[0, pl.ds(off, chunk), :])
            return (jnp.maximum(carry[0], jnp.max(n1, axis=0, keepdims=True)),
                    jnp.maximum(carry[1], jnp.max(n2, axis=0, keepdims=True)))

        zeros = jnp.zeros((1, LANES), F32)
        k1, k2 = lax.fori_loop(0, k_ref.shape[1] // chunk, body, (zeros, zeros))
        kmax_ref[0:1] = k1
        kmax_ref[1:2] = k2

    q = q_ref[0]
    lane = lax.broadcasted_iota(jnp.int32, q.shape, 1)
    zero = jnp.zeros_like(q)
    qs_ref[0:tq] = jnp.where(lane < DA_HEAD_DIM, q, zero)
    qs_ref[tq:rows] = jnp.where(lane >= DA_HEAD_DIM, q, zero)
    l_ref[...] = jnp.zeros(l_ref.shape, F32)
    acc_ref[...] = jnp.zeros(acc_ref.shape, F32)
    qn1, qn2 = _subhead_sq_norms(q)
    bound1 = jnp.sqrt(qn1 * kmax_ref[0:1])
    bound2 = jnp.sqrt(qn2 * kmax_ref[1:2])
    fixed_shift = jnp.maximum(jnp.max(bound1), jnp.max(bound2)) <= ATTN_SAFE_BOUND

    def scores(tile):
        off = pl.multiple_of(tile * tk, tk)
        return lax.dot_general(qs_ref[...], k_ref[0, pl.ds(off, tk), :], (((1,), (1,)), ((), ())),
                               preferred_element_type=F32)

    def add_pv(tile):
        off = pl.multiple_of(tile * tk, tk)
        acc_ref[...] += jnp.dot(p_ref[...], v_ref[0, pl.ds(off, tk), :], preferred_element_type=F32)

    @pl.when(fixed_shift)
    def _():
        m_ref[0:tq] = bound1
        m_ref[tq:rows] = bound2

        def step(tile, carry):
            s = scores(tile)
            for r0 in range(0, rows, ATTN_ROW_CHUNK):
                rs = slice(r0, r0 + ATTN_ROW_CHUNK)
                m = m_ref[rs]
                psum = None
                for c0 in range(0, tk, LANES):
                    p = jnp.exp2(s[rs, c0:c0 + LANES] - m)
                    psum = p if psum is None else psum + p
                    p_ref[rs, c0:c0 + LANES] = p.astype(BF16)
                l_ref[rs] += psum
            add_pv(tile)
            return carry

        lax.fori_loop(0, n_tiles, step, 0)

    @pl.when(jnp.logical_not(fixed_shift))
    def _():
        m_ref[...] = jnp.full(m_ref.shape, -jnp.inf, F32)

        def step(tile, carry):
            s_ref[...] = scores(tile)
            for r0 in range(0, rows, ATTN_ROW_CHUNK):
                rs = slice(r0, r0 + ATTN_ROW_CHUNK)
                mx = s_ref[rs, 0:LANES]
                for c0 in range(LANES, tk, LANES):
                    mx = jnp.maximum(mx, s_ref[rs, c0:c0 + LANES])
                m_prev = m_ref[rs]
                m_new = jnp.maximum(m_prev, jnp.max(mx, axis=-1, keepdims=True))
                alpha = jnp.exp2(m_prev - m_new)
                psum = None
                for c0 in range(0, tk, LANES):
                    p = jnp.exp2(s_ref[rs, c0:c0 + LANES] - m_new)
                    psum = p if psum is None else psum + p
                    p_ref[rs, c0:c0 + LANES] = p.astype(BF16)
                l_ref[rs] = alpha * l_ref[rs] + psum
                m_ref[rs] = m_new
                acc_ref[rs] = alpha * acc_ref[rs]
            add_pv(tile)
            return carry

        lax.fori_loop(0, n_tiles, step, 0)

    o = acc_ref[...] / jnp.sum(l_ref[...], axis=-1, keepdims=True)
    lp = lam_ref[...]
    lam = (jnp.exp(jnp.sum(lp[0:1] * lp[1:2], axis=-1, keepdims=True))
           - jnp.exp(jnp.sum(lp[2:3] * lp[3:4], axis=-1, keepdims=True)) + lambda_init)
    diff = o[0:tq] - lam * o[tq:2 * tq]
    y = diff * lax.rsqrt(jnp.mean(diff * diff, axis=-1, keepdims=True) + EPS) * g_ref[...]
    o_ref[0] = (y * (1.0 - lambda_init)).astype(BF16)


def _diff_attention(q, k, v, lam_params, subln_g, lambda_init):
    b, s, d = q.shape
    hw = 2 * DA_HEAD_DIM
    nh = d // hw
    tq = _token_tile(s, 1024)
    tk = _token_tile(s, 2048)
    kern = functools.partial(_attn_kernel, lambda_init=lambda_init, tq=tq, tk=tk)
    return pl.pallas_call(
        kern,
        grid=(b, nh, s // tq),
        in_specs=[
            _const_spec((4, DA_HEAD_DIM)),
            pl.BlockSpec((1, tq, hw), lambda bi, hi, qi: (bi, qi, hi)),
            pl.BlockSpec((1, s, hw), lambda bi, hi, qi: (bi, 0, hi)),
            pl.BlockSpec((1, s, hw), lambda bi, hi, qi: (bi, 0, hi)),
            _const_spec((1, hw)),
        ],
        out_specs=pl.BlockSpec((1, tq, hw), lambda bi, hi, qi: (bi, qi, hi)),
        out_shape=jax.ShapeDtypeStruct((b, s, d), BF16),
        scratch_shapes=[
            pltpu.VMEM((2 * tq, hw), BF16),
            pltpu.VMEM((2 * tq, tk), F32),
            pltpu.VMEM((2 * tq, tk), BF16),
            pltpu.VMEM((2 * tq, LANES), F32),
            pltpu.VMEM((2 * tq, LANES), F32),
            pltpu.VMEM((2 * tq, hw), F32),
            pltpu.VMEM((8, LANES), F32),
        ],
        compiler_params=_params("parallel", "parallel", "arbitrary"),
        name="diff_attn",
    )(lam_params, q, k, v, subln_g.reshape(1, hw))


def _outproj_kernel(o_ref, x_ref, mod_ref, w_ref, x1_ref):
    mix = jnp.dot(o_ref[0], w_ref[...], preferred_element_type=F32)
    x1_ref[0] = x_ref[0] + mod_ref[0][2:3] * mix


def _outproj_residual(o, x, modv, w_out_bf16):
    b, s, d = x.shape
    tm = _token_tile(s, 512)
    tok = pl.BlockSpec((1, tm, d), lambda bi, si: (bi, si, 0))
    return pl.pallas_call(
        _outproj_kernel,
        grid=(b, s // tm),
        in_specs=[tok, tok, pl.BlockSpec((1, N_MOD, d), lambda bi, si: (bi, 0, 0)), _const_spec((d, d))],
        out_specs=tok,
        out_shape=jax.ShapeDtypeStruct((b, s, d), F32),
        compiler_params=_params("parallel", "parallel"),
        name="attn_outproj",
    )(o, x, modv, w_out_bf16)


def _sg_kernel(x_ref, mod_ref, g_ref, win_ref, lng_ref, lnb_ref, ws_ref, bs_ref, wout_ref,
               x1_ref, gated_ref):
    x = x_ref[0]
    mod = mod_ref[0]
    tm = x.shape[0]
    half = lng_ref.shape[-1]
    gdim = half // SG_GROUPS
    h = _norm_mod(x, g_ref[...], mod[0:1], mod[1:2])
    z = jnp.dot(h.astype(BF16), win_ref[...], preferred_element_type=F32)
    z = 0.5 * z * (1.0 + lax.erf(z * (2.0 ** -0.5)))
    u = z[:, :half]
    v = z[:, half:]
    mu = jnp.mean(v, axis=-1, keepdims=True)
    vc = v - mu
    v = vc * lax.rsqrt(jnp.mean(vc * vc, axis=-1, keepdims=True) + EPS) * lng_ref[...] + lnb_ref[...]
    vb = v.astype(BF16)
    for c in range(tm // SG_CHUNK):
        r0 = c * SG_CHUNK
        for g in range(SG_GROUPS):
            c0 = g * gdim
            sp = jnp.dot(ws_ref[g], vb[r0:r0 + SG_CHUNK, c0:c0 + gdim],
                         preferred_element_type=F32) + bs_ref[g]
            gated_ref[r0:r0 + SG_CHUNK, c0:c0 + gdim] = (
                u[r0:r0 + SG_CHUNK, c0:c0 + gdim] * sp).astype(BF16)
    mix = jnp.dot(gated_ref[...], wout_ref[...], preferred_element_type=F32)
    x1_ref[0] = x + mod[2:3] * mix


def _spatial_gating(x, modv, norm_g, w_in, ln_g, ln_b, w_s, b_s, w_out):
    b, s, d = x.shape
    half = ln_g.shape[-1]
    tm = _token_tile(s, 512)
    assert tm % SG_CHUNK == 0
    tok = pl.BlockSpec((1, tm, d), lambda bi, si: (bi, si, 0))
    return pl.pallas_call(
        _sg_kernel,
        grid=(b, s // tm),
        in_specs=[
            tok,
            pl.BlockSpec((1, N_MOD, d), lambda bi, si: (bi, 0, 0)),
            _const_spec((1, d)),
            _const_spec((d, 2 * half)),
            _const_spec((1, half)),
            _const_spec((1, half)),
            _const_spec((SG_GROUPS, SG_CHUNK, SG_CHUNK)),
            _const_spec((SG_GROUPS, SG_CHUNK, 1)),
            _const_spec((half, d)),
        ],
        out_specs=tok,
        out_shape=jax.ShapeDtypeStruct((b, s, d), F32),
        scratch_shapes=[pltpu.VMEM((tm, half), BF16)],
        compiler_params=_params("parallel", "parallel"),
        name="spatial_gating",
    )(x, modv, norm_g.reshape(1, d), w_in.astype(BF16), ln_g.reshape(1, half), ln_b.reshape(1, half),
      w_s.astype(BF16), b_s.reshape(SG_GROUPS, SG_CHUNK, 1), w_out.astype(BF16))


def _first_argmax(cur, idx, sentinel):
    m = jnp.max(cur, axis=0, keepdims=True)
    first = jnp.min(jnp.where(cur == m, idx, sentinel), axis=0, keepdims=True)
    return m, idx == first


def _router_kernel(x_ref, mod_ref, g_ref, wrt_ref, bias_ref, upper_ref, lower_ref,
                   h_ref, rank_ref, eid_ref, w_ref, cnt_ref):
    mod = mod_ref[0]
    h = _norm_mod(x_ref[0], g_ref[...], mod[3:4], mod[4:5])
    _rows_to_tiles(h_ref, h)
    logits = lax.dot_general(wrt_ref[...], h, (((1,), (1,)), ((), ())),
                             precision=HIGHEST, preferred_element_type=F32)
    scores = jax.nn.sigmoid(logits)
    sel = scores + bias_ref[...]
    tm = sel.shape[1]
    neg = -jnp.inf
    sub = lax.broadcasted_iota(jnp.int32, (GROUP_SIZE, tm), 0)
    rows = []
    for g in range(N_EXPERT_GROUPS):
        blk = sel[g * GROUP_SIZE:(g + 1) * GROUP_SIZE]
        m1, hit = _first_argmax(blk, sub, GROUP_SIZE)
        m2 = jnp.max(jnp.where(hit, neg, blk), axis=0, keepdims=True)
        rows.append(m1 + m2)
    cur = jnp.concatenate(rows, axis=0)
    gsel = jnp.zeros(cur.shape, jnp.int32)
    for _ in range(TOPK_GROUPS):
        _, hit = _first_argmax(cur, sub, N_EXPERT_GROUPS)
        gsel = jnp.where(hit, 1, gsel)
        cur = jnp.where(hit, neg, cur)
    masked = []
    for g in range(N_EXPERT_GROUPS):
        blk = sel[g * GROUP_SIZE:(g + 1) * GROUP_SIZE]
        masked.append(jnp.where(gsel[g:g + 1] > 0, blk, neg))
    cur = jnp.concatenate(masked, axis=0)
    eidx = lax.broadcasted_iota(jnp.int32, cur.shape, 0)
    chosen = jnp.zeros(cur.shape, jnp.int32)
    for _ in range(TOP_K):
        _, hit = _first_argmax(cur, eidx, N_EXPERTS)
        chosen = jnp.where(hit, 1, chosen)
        cur = jnp.where(hit, neg, cur)
    picked = jnp.where(chosen > 0, scores, 0.0)
    wsum = jnp.sum(picked, axis=0, keepdims=True)
    wdense = picked / wsum * ROUTED_SCALE

    @pl.when((pl.program_id(0) == 0) & (pl.program_id(1) == 0))
    def _():
        cnt_ref[...] = jnp.zeros(cnt_ref.shape, F32)

    cm = chosen.astype(BF16)
    base = cnt_ref[...]
    rank = (jnp.dot(cm, upper_ref[...], preferred_element_type=F32)
            + jnp.concatenate([base] * (tm // LANES), axis=1))
    cnt_ref[...] = base + jnp.dot(cm, jnp.ones((tm, LANES), BF16), preferred_element_type=F32)
    ordinal = jnp.dot(lower_ref[...], cm, preferred_element_type=F32)
    eid = eidx.astype(F32)
    rows_r, rows_e, rows_w = [], [], []
    for k in range(TOP_K):
        pick = (chosen > 0) & (ordinal == k)
        rows_r.append(jnp.sum(jnp.where(pick, rank, 0.0), axis=0, keepdims=True))
        rows_e.append(jnp.sum(jnp.where(pick, eid, 0.0), axis=0, keepdims=True))
        rows_w.append(jnp.sum(jnp.where(pick, wdense, 0.0), axis=0, keepdims=True))
    fill = [jnp.zeros((N_MOD - TOP_K, tm), F32)]
    rank_ref[...] = jnp.concatenate(rows_r + fill, axis=0).astype(jnp.int32)
    eid_ref[...] = jnp.concatenate(rows_e + fill, axis=0).astype(jnp.int32)
    w_ref[...] = jnp.concatenate(rows_w + fill, axis=0)


def _router(x, modv, norm_g, w_router, router_bias, tm):
    b, s, d = x.shape
    t = b * s
    e = w_router.shape[1]
    nst = s // tm
    tok = pl.BlockSpec((1, tm, d), lambda bi, si: (bi, si, 0))
    per_k = pl.BlockSpec((N_MOD, tm), lambda bi, si: (0, bi * nst + si))
    upper = (jnp.arange(tm)[:, None] < jnp.arange(tm)[None, :]).astype(BF16)
    lower = (jnp.arange(e)[None, :] < jnp.arange(e)[:, None]).astype(BF16)
    return pl.pallas_call(
        _router_kernel,
        grid=(b, nst),
        in_specs=[
            tok,
            pl.BlockSpec((1, N_MOD, d), lambda bi, si: (bi, 0, 0)),
            _const_spec((1, d)),
            _const_spec((e, d)),
            _const_spec((e, 1)),
            _const_spec((tm, tm)),
            _const_spec((e, e)),
        ],
        out_specs=[pl.BlockSpec((tm * TOKEN_TILE, TOKEN_W), lambda bi, si: (bi * nst + si, 0)),
                   per_k, per_k, per_k,
                   pl.BlockSpec((e, LANES), lambda bi, si: (0, 0))],
        out_shape=[jax.ShapeDtypeStruct((t * TOKEN_TILE, TOKEN_W), F32),
                   jax.ShapeDtypeStruct((N_MOD, t), jnp.int32),
                   jax.ShapeDtypeStruct((N_MOD, t), jnp.int32),
                   jax.ShapeDtypeStruct((N_MOD, t), F32),
                   jax.ShapeDtypeStruct((e, LANES), F32)],
        compiler_params=_params("arbitrary", "arbitrary"),
        name="moe_router",
    )(x, modv, norm_g.reshape(1, d), w_router.T, router_bias.reshape(e, 1), upper, lower)


def _dispatch_plan(rank, eid, counts, n_blocks, tile):
    e = counts.shape[0]
    t = rank.shape[1]
    padded = (counts + MOE_ROWS - 1) // MOE_ROWS * MOE_ROWS
    pad_end = jnp.cumsum(padded)
    pad_start = pad_end - padded
    start_of = jnp.sum(jnp.where(eid[None] == jnp.arange(e)[:, None, None], pad_start[:, None, None], 0),
                       axis=0)
    dest = (rank + start_of).astype(jnp.int32)
    dest = dest.reshape(TOP_K, t // tile, tile).transpose(1, 0, 2).reshape(t * TOP_K)
    blk_start = jnp.arange(n_blocks, dtype=jnp.int32) * MOE_ROWS
    block_e = jnp.minimum(jnp.sum(pad_end[None, :] <= blk_start[:, None], axis=1), e - 1).astype(jnp.int32)
    slack = padded - counts
    slack_end = jnp.cumsum(slack)
    q = jnp.arange(n_blocks * MOE_ROWS - t * TOP_K, dtype=jnp.int32)
    eq = jnp.sum(slack_end[None, :] <= q[:, None], axis=1)
    ec = jnp.minimum(eq, e - 1)
    in_expert = pad_start[ec] + counts[ec] + q - (slack_end[ec] - slack[ec])
    pad_rows = jnp.where(eq < e, in_expert, pad_end[-1] + q - slack_end[-1]).astype(jnp.int32)
    return dest, block_e, pad_rows


def _row_copy(src, dst, s_tok, d_tok, sem):
    s0 = pl.multiple_of(s_tok * TOKEN_TILE, TOKEN_TILE)
    d0 = pl.multiple_of(d_tok * TOKEN_TILE, TOKEN_TILE)
    return pltpu.make_async_copy(src.at[pl.ds(s0, TOKEN_TILE)], dst.at[pl.ds(d0, TOKEN_TILE)], sem)


def _drain(src, dst, sem, tokens, repeats):
    n = tokens * TOKEN_TILE
    for _ in range(repeats):
        pltpu.make_async_copy(src.at[pl.ds(0, n)], dst.at[pl.ds(0, n)], sem).wait()


def _dispatch_kernel(pads_ref, dest_ref, h_ref, xs_ref, zero_ref, sem, *, tokens, pads):
    def body(i, carry):
        for k in range(TOP_K):
            _row_copy(h_ref, xs_ref, i, dest_ref[k * tokens + i], sem).start(priority=k % 2)
        return carry

    lax.fori_loop(0, tokens, body, 0)
    zero_ref[...] = jnp.zeros(zero_ref.shape, zero_ref.dtype)
    pad0 = pl.program_id(0) * pads

    def fill(i, carry):
        _row_copy(zero_ref, xs_ref, 0, pads_ref[pad0 + i], sem).start()
        return carry

    lax.fori_loop(0, pads, fill, 0)
    _drain(h_ref, xs_ref, sem, tokens, TOP_K + pads // tokens)
    if pads % tokens:
        _drain(h_ref, xs_ref, sem, pads % tokens, 1)


def _dispatch(dest_flat, pad_rows, h, out_tokens, tokens):
    t = h.shape[0] // TOKEN_TILE
    steps = t // tokens
    pads = pad_rows.shape[0] // steps
    assert pads * steps == pad_rows.shape[0]
    return pl.pallas_call(
        functools.partial(_dispatch_kernel, tokens=tokens, pads=pads),
        grid_spec=pltpu.PrefetchScalarGridSpec(
            num_scalar_prefetch=1,
            grid=(steps,),
            in_specs=[
                pl.BlockSpec((tokens * TOP_K,), lambda i, pr: (i,), memory_space=pltpu.SMEM),
                pl.BlockSpec((tokens * TOKEN_TILE, TOKEN_W), lambda i, pr: (i, 0)),
            ],
            out_specs=pl.BlockSpec(memory_space=pl.ANY),
            scratch_shapes=[pltpu.VMEM((TOKEN_TILE, TOKEN_W), h.dtype), pltpu.SemaphoreType.DMA(())],
        ),
        out_shape=jax.ShapeDtypeStruct((out_tokens * TOKEN_TILE, TOKEN_W), h.dtype),
        compiler_params=_params("arbitrary"),
        name="moe_dispatch",
    )(pad_rows, dest_flat, h)


def _expert_kernel(be_ref, xs_ref, wg_ref, wu_ref, wd_ref, o_ref, wgu_s, wd_s):
    i = pl.program_id(0)
    f = wd_s.shape[0]

    @pl.when((i == 0) | (be_ref[i] != be_ref[jnp.maximum(i - 1, 0)]))
    def _():
        wgu_s[:, 0:f] = wg_ref[0].astype(BF16)
        wgu_s[:, f:2 * f] = wu_ref[0].astype(BF16)
        wd_s[...] = wd_ref[0].astype(BF16)

    xb = _rows_from_tiles(xs_ref, MOE_ROWS, TOKEN_TILE).astype(BF16)
    gu = jnp.dot(xb, wgu_s[...], preferred_element_type=F32)
    gate = gu[:, 0:f]
    up = gu[:, f:2 * f]
    hb = (gate * jax.nn.sigmoid(gate) * up).astype(BF16)
    _rows_to_tiles(o_ref, jnp.dot(hb, wd_s[...], preferred_element_type=F32))


def _experts(xs, block_e, w_gate, w_up, w_down, layer):
    d, f = w_gate.shape[2:]
    assert d == TOKEN_TILE * TOKEN_W
    nb = xs.shape[0] // (MOE_ROWS * TOKEN_TILE)
    rows = pl.BlockSpec((MOE_ROWS * TOKEN_TILE, TOKEN_W), lambda i, be: (i, 0))
    return pl.pallas_call(
        _expert_kernel,
        grid_spec=pltpu.PrefetchScalarGridSpec(
            num_scalar_prefetch=1,
            grid=(nb,),
            in_specs=[
                rows,
                pl.BlockSpec((None, 1, d, f), lambda i, be: (layer, be[i], 0, 0)),
                pl.BlockSpec((None, 1, d, f), lambda i, be: (layer, be[i], 0, 0)),
                pl.BlockSpec((None, 1, f, d), lambda i, be: (layer, be[i], 0, 0)),
            ],
            out_specs=rows,
            scratch_shapes=[pltpu.VMEM((d, 2 * f), BF16), pltpu.VMEM((f, d), BF16)],
        ),
        out_shape=jax.ShapeDtypeStruct(xs.shape, F32),
        compiler_params=_params("arbitrary"),
        name="moe_experts",
    )(block_e, xs, w_gate, w_up, w_down)


def _combine_kernel(dest_ref, x_ref, h_ref, eo_ref, w_ref, mod_ref, sg_ref, su_ref, sd_ref, fg_ref, o_ref,
                    buf_ref, sem, *, final_norm):
    tm = x_ref.shape[1]

    def gather(i, carry):
        for k in range(TOP_K):
            _row_copy(eo_ref, buf_ref.at[k], dest_ref[k * tm + i], i, sem).start(priority=k % 2)
        return carry

    lax.fori_loop(0, tm, gather, 0)
    hb = _rows_from_tiles(h_ref, tm, TOKEN_TILE).astype(BF16)
    gate = jnp.dot(hb, sg_ref[...], preferred_element_type=F32)
    up = jnp.dot(hb, su_ref[...], preferred_element_type=F32)
    y = jnp.dot((gate * jax.nn.sigmoid(gate) * up).astype(BF16), sd_ref[...], preferred_element_type=F32)
    w = w_ref[...]
    _drain(eo_ref, buf_ref.at[0], sem, tm, TOP_K)
    for k in range(TOP_K):
        y = y + _rows_from_tiles(buf_ref, tm, TOKEN_TILE, lead=(k,)) * w[:, k:k + 1]
    out = x_ref[0] + mod_ref[0][5:6] * y
    if final_norm:
        out = out * lax.rsqrt(jnp.mean(out * out, axis=-1, keepdims=True) + EPS) * fg_ref[...]
    o_ref[0] = out


def _combine(dest_flat, x, h, eo, wts, modv, ws_gate, ws_up, ws_down, final_g, final_norm, tm):
    b, s, d = x.shape
    f = ws_gate.shape[-1]
    nst = s // tm
    tok = pl.BlockSpec((1, tm, d), lambda bi, si: (bi, si, 0))
    tr = tm * TOKEN_TILE
    return pl.pallas_call(
        functools.partial(_combine_kernel, final_norm=final_norm),
        grid=(b, nst),
        in_specs=[
            pl.BlockSpec((tm * TOP_K,), lambda bi, si: (bi * nst + si,), memory_space=pltpu.SMEM),
            tok,
            pl.BlockSpec((tr, TOKEN_W), lambda bi, si: (bi * nst + si, 0)),
            pl.BlockSpec(memory_space=pl.ANY),
            pl.BlockSpec((tm, TOP_K), lambda bi, si: (bi * nst + si, 0)),
            pl.BlockSpec((1, N_MOD, d), lambda bi, si: (bi, 0, 0)),
            _const_spec((d, f)), _const_spec((d, f)), _const_spec((f, d)),
            _const_spec((1, d)),
        ],
        out_specs=tok,
        out_shape=jax.ShapeDtypeStruct((b, s, d), F32),
        scratch_shapes=[pltpu.VMEM((TOP_K, tr, TOKEN_W), F32), pltpu.SemaphoreType.DMA(())],
        compiler_params=_params("arbitrary", "arbitrary"),
        name="moe_combine",
    )(dest_flat, x, h, eo, wts, modv, ws_gate.astype(BF16), ws_up.astype(BF16), ws_down.astype(BF16),
      final_g.reshape(1, d))


def _moe_layer(x, modv, norm_g, w_router, router_bias, w_gate, w_up, w_down, layer,
               ws_gate, ws_up, ws_down, final_g, final_norm):
    b, s, d = x.shape
    t = b * s
    nb = pl.cdiv(t * TOP_K, MOE_ROWS) + N_EXPERTS
    tile = _token_tile(s, 512)
    h, rank, eid, w6, cnt = _router(x, modv, norm_g, w_router, router_bias, tile)
    counts = cnt[:, 0].astype(jnp.int32)
    dest_flat, block_e, pad_rows = _dispatch_plan(rank[:TOP_K], eid[:TOP_K], counts, nb, tile)
    wts = w6[:TOP_K].T
    xs = _dispatch(dest_flat, pad_rows, h, nb * MOE_ROWS, tile)
    eo = _experts(xs, block_e, w_gate, w_up, w_down, layer)
    return _combine(dest_flat, x, h, eo, wts, modv, ws_gate, ws_up, ws_down, final_g, final_norm, tile)


def kernel(x, c, positions, norm1_g, norm2_g, ada_w, ada_b, da_w_in, da_lam_q1, da_lam_k1, da_lam_q2, da_lam_k2, da_subln_g, da_w_out, sg_w_in, sg_ln_g, sg_ln_b, sg_w_s, sg_b_s, sg_w_out, moe_w_router, moe_router_bias, moe_w_gate, moe_w_up, moe_w_down, moe_ws_gate, moe_ws_up, moe_ws_down, final_g):
    depth = ada_w.shape[0]
    mod = _ada_mod(c, ada_w, ada_b)
    for i in range(depth):
        j = i // 2
        if i % 2 == 0:
            lambda_init = 0.8 - 0.6 * math.exp(-0.3 * i)
            q, k, v = _qkv_rope(x, mod[i], norm1_g[i], positions, da_w_in[j].astype(BF16))
            lam_params = jnp.stack([da_lam_q1[j], da_lam_k1[j], da_lam_q2[j], da_lam_k2[j]])
            o = _diff_attention(q, k, v, lam_params, da_subln_g[j], lambda_init)
            x = _outproj_residual(o, x, mod[i], da_w_out[j].astype(BF16))
        else:
            x = _spatial_gating(x, mod[i], norm1_g[i], sg_w_in[j], sg_ln_g[j], sg_ln_b[j],
                                sg_w_s[j], sg_b_s[j], sg_w_out[j])
        x = _moe_layer(x, mod[i], norm2_g[i], moe_w_router[i], moe_router_bias[i],
                       moe_w_gate, moe_w_up, moe_w_down, i,
                       moe_ws_gate[i], moe_ws_up[i], moe_ws_down[i],
                       final_g, final_norm=(i == depth - 1))
    return x
```

```python
import functools
import math

import jax
import jax.numpy as jnp
from jax import lax
from jax.experimental import pallas as pl
from jax.experimental.pallas import tpu as pltpu

F32 = jnp.float32
BF16 = jnp.bfloat16
HIGHEST = lax.Precision.HIGHEST

EPS = 1e-6
LANES = 128
N_EXPERTS = 64
TOP_K = 6
N_EXPERT_GROUPS = 8
TOPK_GROUPS = 4
GROUP_SIZE = N_EXPERTS // N_EXPERT_GROUPS
ROUTED_SCALE = 2.5
DA_HEAD_DIM = 64
ROPE_THETA = 10000.0
SG_CHUNK = 128
SG_GROUPS = 8
N_MOD = 8

MOE_ROWS = 512
TOKEN_W = 128
TOKEN_TILE = 8
VMEM_LIMIT = 56 * 1024 * 1024


def _params(*sem):
    return pltpu.CompilerParams(dimension_semantics=sem, vmem_limit_bytes=VMEM_LIMIT)


def _const_spec(shape):
    n = len(shape)
    return pl.BlockSpec(shape, lambda *_: (0,) * n, pipeline_mode=pl.Buffered(1))


def _token_tile(s, target):
    t = min(s, target)
    assert s % t == 0
    return t


def _rows_to_tiles(ref, val):
    n, d = val.shape
    per = d // TOKEN_W
    for i in range(per):
        ref[pl.ds(i, n, stride=per), :] = val[:, i * TOKEN_W:(i + 1) * TOKEN_W]


def _rows_from_tiles(ref, n, per, lead=()):
    return jnp.concatenate([ref[lead + (pl.ds(i, n, stride=per), slice(None))] for i in range(per)],
                           axis=-1)


def _norm_mod(x, g, shift, scale):
    y = x * lax.rsqrt(jnp.mean(x * x, axis=-1, keepdims=True) + EPS) * g
    return y * (1.0 + scale) + shift


def _mod_kernel(c_ref, w_ref, b_ref, o_ref):
    c = c_ref[...]
    cond = c * jax.nn.sigmoid(c)
    o_ref[0] = jnp.dot(cond, w_ref[0], preferred_element_type=F32, precision=HIGHEST) + b_ref[0]


def _ada_mod(c, ada_w, ada_b):
    depth, d, d6 = ada_w.shape
    b = c.shape[0]
    nj = d6 // d
    out = pl.pallas_call(
        _mod_kernel,
        grid=(depth, nj),
        in_specs=[
            pl.BlockSpec((b, d), lambda i, j: (0, 0)),
            pl.BlockSpec((1, d, d), lambda i, j: (i, 0, j)),
            pl.BlockSpec((1, 1, d), lambda i, j: (i, 0, j)),
        ],
        out_specs=pl.BlockSpec((1, b, d), lambda i, j: (i, 0, j)),
        out_shape=jax.ShapeDtypeStruct((depth, b, d6), F32),
        compiler_params=_params("arbitrary", "arbitrary"),
        name="ada_mod",
    )(c, ada_w, ada_b.reshape(depth, 1, d6))
    mod = out.reshape(depth, b, nj, d)
    return jnp.pad(mod, ((0, 0), (0, 0), (0, N_MOD - nj), (0, 0)))


def _qkv_kernel(x_ref, mod_ref, g_ref, pos_ref, freq_ref, w_ref, q_ref, k_ref, v_ref):
    x = x_ref[0]
    mod = mod_ref[0]
    d = x.shape[-1]
    h = _norm_mod(x, g_ref[...], mod[0:1], mod[1:2])
    qkv = jnp.dot(h.astype(BF16), w_ref[...], preferred_element_type=F32)
    ang = pos_ref[0].astype(F32) * freq_ref[...]
    cos = jnp.cos(ang)
    sin = jnp.sin(ang)
    lane = lax.broadcasted_iota(jnp.int32, ang.shape, 1)
    first_half = (lane % DA_HEAD_DIM) < (DA_HEAD_DIM // 2)
    sin_signed = jnp.where(first_half, -sin, sin)
    half = DA_HEAD_DIM // 2

    def rope(blk):
        partner = jnp.where(first_half, pltpu.roll(blk, LANES - half, 1), pltpu.roll(blk, half, 1))
        return blk * cos + partner * sin_signed

    q_scale = DA_HEAD_DIM ** -0.5 * math.log2(math.e)
    for cb in range(d // LANES):
        lo = cb * LANES
        q_ref[0, :, lo:lo + LANES] = (rope(qkv[:, lo:lo + LANES]) * q_scale).astype(BF16)
        k_ref[0, :, lo:lo + LANES] = rope(qkv[:, d + lo:d + lo + LANES]).astype(BF16)
    v_ref[0] = qkv[:, 2 * d:].astype(BF16)


def _qkv_rope(x, modv, norm_g, positions, w_in_bf16):
    b, s, d = x.shape
    tm = _token_tile(s, 512)
    inv_freq = ROPE_THETA ** (-jnp.arange(0, DA_HEAD_DIM, 2, dtype=F32) / DA_HEAD_DIM)
    freq = jnp.tile(inv_freq, LANES // (DA_HEAD_DIM // 2)).reshape(1, LANES)
    tok = pl.BlockSpec((1, tm, d), lambda bi, si: (bi, si, 0))
    out = jax.ShapeDtypeStruct((b, s, d), BF16)
    return pl.pallas_call(
        _qkv_kernel,
        grid=(b, s // tm),
        in_specs=[
            tok,
            pl.BlockSpec((1, N_MOD, d), lambda bi, si: (bi, 0, 0)),
            _const_spec((1, d)),
            pl.BlockSpec((1, tm, 1), lambda bi, si: (bi, si, 0)),
            _const_spec((1, LANES)),
            _const_spec((d, 3 * d)),
        ],
        out_specs=[tok, tok, tok],
        out_shape=[out, out, out],
        compiler_params=_params("parallel", "parallel"),
        name="qkv_rope",
    )(x, modv, norm_g.reshape(1, d), positions.reshape(b, s, 1), freq, w_in_bf16)


ATTN_ROW_CHUNK = 64


ATTN_NORM_CHUNK = 1024
ATTN_SAFE_BOUND = 50.0


def _subhead_sq_norms(x):
    xf = x.astype(F32)
    sq = (xf * xf).astype(BF16)
    row = lax.broadcasted_iota(jnp.int32, (LANES, LANES), 0)
    first = (row < DA_HEAD_DIM).astype(BF16)
    second = (row >= DA_HEAD_DIM).astype(BF16)
    return (jnp.dot(sq, first, preferred_element_type=F32),
            jnp.dot(sq, second, preferred_element_type=F32))


def _attn_kernel(lam_ref, q_ref, k_ref, v_ref, g_ref, o_ref, qs_ref, s_ref, p_ref,
                 m_ref, l_ref, acc_ref, kmax_ref, *, lambda_init, tq, tk):
    rows = 2 * tq
    n_tiles = k_ref.shape[1] // tk

    @pl.when(pl.program_id(2) == 0)
    def _():
        chunk = min(ATTN_NORM_CHUNK, k_ref.shape[1])

        def body(c, carry):
            off = pl.multiple_of(c * chunk, chunk)
            n1, n2 = _subhead_sq_norms(k_ref[0, pl.ds(off, chunk), :])
            return (jnp.maximum(carry[0], jnp.max(n1, axis=0, keepdims=True)),
                    jnp.maximum(carry[1], jnp.max(n2, axis=0, keepdims=True)))

        zeros = jnp.zeros((1, LANES), F32)
        k1, k2 = lax.fori_loop(0, k_ref.shape[1] // chunk, body, (zeros, zeros))
        kmax_ref[0:1] = k1
        kmax_ref[1:2] = k2

    q = q_ref[0]
    lane = lax.broadcasted_iota(jnp.int32, q.shape, 1)
    zero = jnp.zeros_like(q)
    qs_ref[0:tq] = jnp.where(lane < DA_HEAD_DIM, q, zero)
    qs_ref[tq:rows] = jnp.where(lane >= DA_HEAD_DIM, q, zero)
    l_ref[...] = jnp.zeros(l_ref.shape, F32)
    acc_ref[...] = jnp.zeros(acc_ref.shape, F32)
    qn1, qn2 = _subhead_sq_norms(q)
    bound1 = jnp.sqrt(qn1 * kmax_ref[0:1])
    bound2 = jnp.sqrt(qn2 * kmax_ref[1:2])
    fixed_shift = jnp.maximum(jnp.max(bound1), jnp.max(bound2)) <= ATTN_SAFE_BOUND

    def scores(tile):
        off = pl.multiple_of(tile * tk, tk)
        return lax.dot_general(qs_ref[...], k_ref[0, pl.ds(off, tk), :], (((1,), (1,)), ((), ())),
                               preferred_element_type=F32)

    def add_pv(tile):
        off = pl.multiple_of(tile * tk, tk)
        acc_ref[...] += jnp.dot(p_ref[...], v_ref[0, pl.ds(off, tk), :], preferred_element_type=F32)

    @pl.when(fixed_shift)
    def _():
        m_ref[0:tq] = bound1
        m_ref[tq:rows] = bound2

        def step(tile, carry):
            s = scores(tile)
            for r0 in range(0, rows, ATTN_ROW_CHUNK):
                rs = slice(r0, r0 + ATTN_ROW_CHUNK)
                m = m_ref[rs]
                psum = None
                for c0 in range(0, tk, LANES):
                    p = jnp.exp2(s[rs, c0:c0 + LANES] - m)
                    psum = p if psum is None else psum + p
                    p_ref[rs, c0:c0 + LANES] = p.astype(BF16)
                l_ref[rs] += psum
            add_pv(tile)
            return carry

        lax.fori_loop(0, n_tiles, step, 0)

    @pl.when(jnp.logical_not(fixed_shift))
    def _():
        m_ref[...] = jnp.full(m_ref.shape, -jnp.inf, F32)

        def step(tile, carry):
            s_ref[...] = scores(tile)
            for r0 in range(0, rows, ATTN_ROW_CHUNK):
                rs = slice(r0, r0 + ATTN_ROW_CHUNK)
                mx = s_ref[rs, 0:LANES]
                for c0 in range(LANES, tk, LANES):
                    mx = jnp.maximum(mx, s_ref[rs, c0:c0 + LANES])
                m_prev = m_ref[rs]
                m_new = jnp.maximum(m_prev, jnp.max(mx, axis=-1, keepdims=True))
                alpha = jnp.exp2(m_prev - m_new)
                psum = None
                for c0 in range(0, tk, LANES):
                    p = jnp.exp2(s_ref[rs, c0:c0 + LANES] - m_new)
                    psum = p if psum is None else psum + p
                    p_ref[rs, c0:c0 + LANES] = p.astype(BF16)
                l_ref[rs] = alpha * l_ref[rs] + psum
                m_ref[rs] = m_new
                acc_ref[rs] = alpha * acc_ref[rs]
            add_pv(tile)
            return carry

        lax.fori_loop(0, n_tiles, step, 0)

    o = acc_ref[...] / jnp.sum(l_ref[...], axis=-1, keepdims=True)
    lp = lam_ref[...]
    lam = (jnp.exp(jnp.sum(lp[0:1] * lp[1:2], axis=-1, keepdims=True))
           - jnp.exp(jnp.sum(lp[2:3] * lp[3:4], axis=-1, keepdims=True)) + lambda_init)
    diff = o[0:tq] - lam * o[tq:2 * tq]
    y = diff * lax.rsqrt(jnp.mean(diff * diff, axis=-1, keepdims=True) + EPS) * g_ref[...]
    o_ref[0] = (y * (1.0 - lambda_init)).astype(BF16)


def _diff_attention(q, k, v, lam_params, subln_g, lambda_init):
    b, s, d = q.shape
    hw = 2 * DA_HEAD_DIM
    nh = d // hw
    tq = _token_tile(s, 1024)
    tk = _token_tile(s, 2048)
    kern = functools.partial(_attn_kernel, lambda_init=lambda_init, tq=tq, tk=tk)
    return pl.pallas_call(
        kern,
        grid=(b, nh, s // tq),
        in_specs=[
            _const_spec((4, DA_HEAD_DIM)),
            pl.BlockSpec((1, tq, hw), lambda bi, hi, qi: (bi, qi, hi)),
            pl.BlockSpec((1, s, hw), lambda bi, hi, qi: (bi, 0, hi)),
            pl.BlockSpec((1, s, hw), lambda bi, hi, qi: (bi, 0, hi)),
            _const_spec((1, hw)),
        ],
        out_specs=pl.BlockSpec((1, tq, hw), lambda bi, hi, qi: (bi, qi, hi)),
        out_shape=jax.ShapeDtypeStruct((b, s, d), BF16),
        scratch_shapes=[
            pltpu.VMEM((2 * tq, hw), BF16),
            pltpu.VMEM((2 * tq, tk), F32),
            pltpu.VMEM((2 * tq, tk), BF16),
            pltpu.VMEM((2 * tq, LANES), F32),
            pltpu.VMEM((2 * tq, LANES), F32),
            pltpu.VMEM((2 * tq, hw), F32),
            pltpu.VMEM((8, LANES), F32),
        ],
        compiler_params=_params("parallel", "parallel", "arbitrary"),
        name="diff_attn",
    )(lam_params, q, k, v, subln_g.reshape(1, hw))


def _sg_kernel(x_ref, mod_ref, g_ref, win_ref, lng_ref, lnb_ref, ws_ref, bs_ref, wout_ref,
               x1_ref, gated_ref):
    x = x_ref[0]
    mod = mod_ref[0]
    tm = x.shape[0]
    half = lng_ref.shape[-1]
    gdim = half // SG_GROUPS
    h = _norm_mod(x, g_ref[...], mod[0:1], mod[1:2])
    z = jnp.dot(h.astype(BF16), win_ref[...], preferred_element_type=F32)
    z = 0.5 * z * (1.0 + lax.erf(z * (2.0 ** -0.5)))
    u = z[:, :half]
    v = z[:, half:]
    mu = jnp.mean(v, axis=-1, keepdims=True)
    vc = v - mu
    v = vc * lax.rsqrt(jnp.mean(vc * vc, axis=-1, keepdims=True) + EPS) * lng_ref[...] + lnb_ref[...]
    vb = v.astype(BF16)
    for c in range(tm // SG_CHUNK):
        r0 = c * SG_CHUNK
        for g in range(SG_GROUPS):
            c0 = g * gdim
            sp = jnp.dot(ws_ref[g], vb[r0:r0 + SG_CHUNK, c0:c0 + gdim],
                         preferred_element_type=F32) + bs_ref[g]
            gated_ref[r0:r0 + SG_CHUNK, c0:c0 + gdim] = (
                u[r0:r0 + SG_CHUNK, c0:c0 + gdim] * sp).astype(BF16)
    mix = jnp.dot(gated_ref[...], wout_ref[...], preferred_element_type=F32)
    x1_ref[0] = x + mod[2:3] * mix


def _spatial_gating(x, modv, norm_g, w_in, ln_g, ln_b, w_s, b_s, w_out):
    b, s, d = x.shape
    half = ln_g.shape[-1]
    tm = _token_tile(s, 512)
    assert tm % SG_CHUNK == 0
    tok = pl.BlockSpec((1, tm, d), lambda bi, si: (bi, si, 0))
    return pl.pallas_call(
        _sg_kernel,
        grid=(b, s // tm),
        in_specs=[
            tok,
            pl.BlockSpec((1, N_MOD, d), lambda bi, si: (bi, 0, 0)),
            _const_spec((1, d)),
            _const_spec((d, 2 * half)),
            _const_spec((1, half)),
            _const_spec((1, half)),
            _const_spec((SG_GROUPS, SG_CHUNK, SG_CHUNK)),
            _const_spec((SG_GROUPS, SG_CHUNK, 1)),
            _const_spec((half, d)),
        ],
        out_specs=tok,
        out_shape=jax.ShapeDtypeStruct((b, s, d), F32),
        scratch_shapes=[pltpu.VMEM((tm, half), BF16)],
        compiler_params=_params("parallel", "parallel"),
        name="spatial_gating",
    )(x, modv, norm_g.reshape(1, d), w_in.astype(BF16), ln_g.reshape(1, half), ln_b.reshape(1, half),
      w_s.astype(BF16), b_s.reshape(SG_GROUPS, SG_CHUNK, 1), w_out.astype(BF16))


def _first_argmax(cur, idx, sentinel):
    m = jnp.max(cur, axis=0, keepdims=True)
    first = jnp.min(jnp.where(cur == m, idx, sentinel), axis=0, keepdims=True)
    return m, idx == first


def _router_kernel(x_ref, mod_ref, g_ref, wrt_ref, bias_ref, upper_ref, lower_ref, *rest, mixer_out):
    mod = mod_ref[0]
    x = x_ref[0]
    if mixer_out:
        o_ref, wout_ref, h_ref, rank_ref, eid_ref, w_ref, cnt_ref, x1_ref = rest
        x = x + mod[2:3] * jnp.dot(o_ref[0], wout_ref[...], preferred_element_type=F32)
        x1_ref[0] = x
    else:
        h_ref, rank_ref, eid_ref, w_ref, cnt_ref = rest
    h = _norm_mod(x, g_ref[...], mod[3:4], mod[4:5])
    _rows_to_tiles(h_ref, h)
    logits = lax.dot_general(wrt_ref[...], h, (((1,), (1,)), ((), ())),
                             precision=HIGHEST, preferred_element_type=F32)
    scores = jax.nn.sigmoid(logits)
    sel = scores + bias_ref[...]
    tm = sel.shape[1]
    neg = -jnp.inf
    sub = lax.broadcasted_iota(jnp.int32, (GROUP_SIZE, tm), 0)
    rows = []
    for g in range(N_EXPERT_GROUPS):
        blk = sel[g * GROUP_SIZE:(g + 1) * GROUP_SIZE]
        m1, hit = _first_argmax(blk, sub, GROUP_SIZE)
        m2 = jnp.max(jnp.where(hit, neg, blk), axis=0, keepdims=True)
        rows.append(m1 + m2)
    cur = jnp.concatenate(rows, axis=0)
    gsel = jnp.zeros(cur.shape, jnp.int32)
    for _ in range(TOPK_GROUPS):
        _, hit = _first_argmax(cur, sub, N_EXPERT_GROUPS)
        gsel = jnp.where(hit, 1, gsel)
        cur = jnp.where(hit, neg, cur)
    masked = []
    for g in range(N_EXPERT_GROUPS):
        blk = sel[g * GROUP_SIZE:(g + 1) * GROUP_SIZE]
        masked.append(jnp.where(gsel[g:g + 1] > 0, blk, neg))
    cur = jnp.concatenate(masked, axis=0)
    eidx = lax.broadcasted_iota(jnp.int32, cur.shape, 0)
    chosen = jnp.zeros(cur.shape, jnp.int32)
    for _ in range(TOP_K):
        _, hit = _first_argmax(cur, eidx, N_EXPERTS)
        chosen = jnp.where(hit, 1, chosen)
        cur = jnp.where(hit, neg, cur)
    picked = jnp.where(chosen > 0, scores, 0.0)
    wsum = jnp.sum(picked, axis=0, keepdims=True)
    wdense = picked / wsum * ROUTED_SCALE

    @pl.when((pl.program_id(0) == 0) & (pl.program_id(1) == 0))
    def _():
        cnt_ref[...] = jnp.zeros(cnt_ref.shape, F32)

    cm = chosen.astype(BF16)
    base = cnt_ref[...]
    rank = (jnp.dot(cm, upper_ref[...], preferred_element_type=F32)
            + jnp.concatenate([base] * (tm // LANES), axis=1))
    cnt_ref[...] = base + jnp.dot(cm, jnp.ones((tm, LANES), BF16), preferred_element_type=F32)
    ordinal = jnp.dot(lower_ref[...], cm, preferred_element_type=F32)
    eid = eidx.astype(F32)
    rows_r, rows_e, rows_w = [], [], []
    for k in range(TOP_K):
        pick = (chosen > 0) & (ordinal == k)
        rows_r.append(jnp.sum(jnp.where(pick, rank, 0.0), axis=0, keepdims=True))
        rows_e.append(jnp.sum(jnp.where(pick, eid, 0.0), axis=0, keepdims=True))
        rows_w.append(jnp.sum(jnp.where(pick, wdense, 0.0), axis=0, keepdims=True))
    fill = [jnp.zeros((N_MOD - TOP_K, tm), F32)]
    rank_ref[...] = jnp.concatenate(rows_r + fill, axis=0).astype(jnp.int32)
    eid_ref[...] = jnp.concatenate(rows_e + fill, axis=0).astype(jnp.int32)
    w_ref[...] = jnp.concatenate(rows_w + fill, axis=0)


def _router(x, modv, norm_g, w_router, router_bias, tm, mixer_out=None):
    b, s, d = x.shape
    t = b * s
    e = w_router.shape[1]
    nst = s // tm
    tok = pl.BlockSpec((1, tm, d), lambda bi, si: (bi, si, 0))
    per_k = pl.BlockSpec((N_MOD, tm), lambda bi, si: (0, bi * nst + si))
    upper = (jnp.arange(tm)[:, None] < jnp.arange(tm)[None, :]).astype(BF16)
    lower = (jnp.arange(e)[None, :] < jnp.arange(e)[:, None]).astype(BF16)
    in_specs = [
        tok,
        pl.BlockSpec((1, N_MOD, d), lambda bi, si: (bi, 0, 0)),
        _const_spec((1, d)),
        _const_spec((e, d)),
        _const_spec((e, 1)),
        _const_spec((tm, tm)),
        _const_spec((e, e)),
    ]
    args = [x, modv, norm_g.reshape(1, d), w_router.T, router_bias.reshape(e, 1), upper, lower]
    out_specs = [pl.BlockSpec((tm * TOKEN_TILE, TOKEN_W), lambda bi, si: (bi * nst + si, 0)),
                 per_k, per_k, per_k,
                 pl.BlockSpec((e, LANES), lambda bi, si: (0, 0))]
    out_shape = [jax.ShapeDtypeStruct((t * TOKEN_TILE, TOKEN_W), F32),
                 jax.ShapeDtypeStruct((N_MOD, t), jnp.int32),
                 jax.ShapeDtypeStruct((N_MOD, t), jnp.int32),
                 jax.ShapeDtypeStruct((N_MOD, t), F32),
                 jax.ShapeDtypeStruct((e, LANES), F32)]
    if mixer_out is not None:
        in_specs += [tok, _const_spec((d, d))]
        args += list(mixer_out)
        out_specs.append(tok)
        out_shape.append(jax.ShapeDtypeStruct((b, s, d), F32))
    return pl.pallas_call(
        functools.partial(_router_kernel, mixer_out=mixer_out is not None),
        grid=(b, nst),
        in_specs=in_specs,
        out_specs=out_specs,
        out_shape=out_shape,
        compiler_params=_params("arbitrary", "arbitrary"),
        name="moe_router",
    )(*args)


def _dispatch_plan(rank, eid, counts, n_blocks, tile):
    e = counts.shape[0]
    t = rank.shape[1]
    padded = (counts + MOE_ROWS - 1) // MOE_ROWS * MOE_ROWS
    pad_end = jnp.cumsum(padded)
    pad_start = pad_end - padded
    start_of = jnp.sum(jnp.where(eid[None] == jnp.arange(e)[:, None, None], pad_start[:, None, None], 0),
                       axis=0)
    dest = (rank + start_of).astype(jnp.int32)
    dest = dest.reshape(TOP_K, t // tile, tile).transpose(1, 0, 2).reshape(t * TOP_K)
    blk_start = jnp.arange(n_blocks, dtype=jnp.int32) * MOE_ROWS
    block_e = jnp.minimum(jnp.sum(pad_end[None, :] <= blk_start[:, None], axis=1), e - 1).astype(jnp.int32)
    slack = padded - counts
    slack_end = jnp.cumsum(slack)
    q = jnp.arange(n_blocks * MOE_ROWS - t * TOP_K, dtype=jnp.int32)
    eq = jnp.sum(slack_end[None, :] <= q[:, None], axis=1)
    first_free = pad_start + counts - (slack_end - slack)
    in_expert = q + jnp.sum(jnp.where(eq[None] == jnp.arange(e)[:, None], first_free[:, None], 0), axis=0)
    pad_rows = jnp.where(eq < e, in_expert, pad_end[-1] + q - slack_end[-1]).astype(jnp.int32)
    return dest, block_e, pad_rows


def _row_copy(src, dst, s_tok, d_tok, sem):
    s0 = pl.multiple_of(s_tok * TOKEN_TILE, TOKEN_TILE)
    d0 = pl.multiple_of(d_tok * TOKEN_TILE, TOKEN_TILE)
    return pltpu.make_async_copy(src.at[pl.ds(s0, TOKEN_TILE)], dst.at[pl.ds(d0, TOKEN_TILE)], sem)


def _drain(src, dst, sem, tokens, repeats):
    n = tokens * TOKEN_TILE
    for _ in range(repeats):
        pltpu.make_async_copy(src.at[pl.ds(0, n)], dst.at[pl.ds(0, n)], sem).wait()


def _dispatch_kernel(pads_ref, dest_ref, h_ref, xs_ref, zero_ref, sem, *, tokens, pads):
    def body(i, carry):
        for k in range(TOP_K):
            _row_copy(h_ref, xs_ref, i, dest_ref[k * tokens + i], sem).start(priority=k % 2)
        return carry

    lax.fori_loop(0, tokens, body, 0)
    zero_ref[...] = jnp.zeros(zero_ref.shape, zero_ref.dtype)
    pad0 = pl.program_id(0) * pads

    def fill(i, carry):
        _row_copy(zero_ref, xs_ref, 0, pads_ref[pad0 + i], sem).start()
        return carry

    lax.fori_loop(0, pads, fill, 0)
    _drain(h_ref, xs_ref, sem, tokens, TOP_K + pads // tokens)
    if pads % tokens:
        _drain(h_ref, xs_ref, sem, pads % tokens, 1)


def _dispatch(dest_flat, pad_rows, h, out_tokens, tokens):
    t = h.shape[0] // TOKEN_TILE
    steps = t // tokens
    pads = pad_rows.shape[0] // steps
    assert pads * steps == pad_rows.shape[0]
    return pl.pallas_call(
        functools.partial(_dispatch_kernel, tokens=tokens, pads=pads),
        grid_spec=pltpu.PrefetchScalarGridSpec(
            num_scalar_prefetch=1,
            grid=(steps,),
            in_specs=[
                pl.BlockSpec((tokens * TOP_K,), lambda i, pr: (i,), memory_space=pltpu.SMEM),
                pl.BlockSpec((tokens * TOKEN_TILE, TOKEN_W), lambda i, pr: (i, 0)),
            ],
            out_specs=pl.BlockSpec(memory_space=pl.ANY),
            scratch_shapes=[pltpu.VMEM((TOKEN_TILE, TOKEN_W), h.dtype), pltpu.SemaphoreType.DMA(())],
        ),
        out_shape=jax.ShapeDtypeStruct((out_tokens * TOKEN_TILE, TOKEN_W), h.dtype),
        compiler_params=_params("arbitrary"),
        name="moe_dispatch",
    )(pad_rows, dest_flat, h)


def _expert_kernel(be_ref, xs_ref, wg_ref, wu_ref, wd_ref, o_ref, wgu_s, wd_s):
    i = pl.program_id(0)
    f = wd_s.shape[0]

    @pl.when((i == 0) | (be_ref[i] != be_ref[jnp.maximum(i - 1, 0)]))
    def _():
        wgu_s[:, 0:f] = wg_ref[0].astype(BF16)
        wgu_s[:, f:2 * f] = wu_ref[0].astype(BF16)
        wd_s[...] = wd_ref[0].astype(BF16)

    xb = _rows_from_tiles(xs_ref, MOE_ROWS, TOKEN_TILE).astype(BF16)
    gu = jnp.dot(xb, wgu_s[...], preferred_element_type=F32)
    gate = gu[:, 0:f]
    up = gu[:, f:2 * f]
    hb = (gate * jax.nn.sigmoid(gate) * up).astype(BF16)
    _rows_to_tiles(o_ref, jnp.dot(hb, wd_s[...], preferred_element_type=F32))


def _experts(xs, block_e, w_gate, w_up, w_down, layer):
    d, f = w_gate.shape[2:]
    assert d == TOKEN_TILE * TOKEN_W
    nb = xs.shape[0] // (MOE_ROWS * TOKEN_TILE)
    rows = pl.BlockSpec((MOE_ROWS * TOKEN_TILE, TOKEN_W), lambda i, be: (i, 0))
    return pl.pallas_call(
        _expert_kernel,
        grid_spec=pltpu.PrefetchScalarGridSpec(
            num_scalar_prefetch=1,
            grid=(nb,),
            in_specs=[
                rows,
                pl.BlockSpec((None, 1, d, f), lambda i, be: (layer, be[i], 0, 0)),
                pl.BlockSpec((None, 1, d, f), lambda i, be: (layer, be[i], 0, 0)),
                pl.BlockSpec((None, 1, f, d), lambda i, be: (layer, be[i], 0, 0)),
            ],
            out_specs=rows,
            scratch_shapes=[pltpu.VMEM((d, 2 * f), BF16), pltpu.VMEM((f, d), BF16)],
        ),
        out_shape=jax.ShapeDtypeStruct(xs.shape, F32),
        compiler_params=_params("arbitrary"),
        name="moe_experts",
    )(block_e, xs, w_gate, w_up, w_down)


def _combine_kernel(dest_ref, x_ref, h_ref, eo_ref, w_ref, mod_ref, sg_ref, su_ref, sd_ref, fg_ref, o_ref,
                    buf_ref, sem, *, final_norm):
    tm = x_ref.shape[1]

    def gather(i, carry):
        for k in range(TOP_K):
            _row_copy(eo_ref, buf_ref.at[k], dest_ref[k * tm + i], i, sem).start(priority=k % 2)
        return carry

    lax.fori_loop(0, tm, gather, 0)
    hb = _rows_from_tiles(h_ref, tm, TOKEN_TILE).astype(BF16)
    gate = jnp.dot(hb, sg_ref[...], preferred_element_type=F32)
    up = jnp.dot(hb, su_ref[...], preferred_element_type=F32)
    y = jnp.dot((gate * jax.nn.sigmoid(gate) * up).astype(BF16), sd_ref[...], preferred_element_type=F32)
    w = w_ref[...]
    _drain(eo_ref, buf_ref.at[0], sem, tm, TOP_K)
    for k in range(TOP_K):
        y = y + _rows_from_tiles(buf_ref, tm, TOKEN_TILE, lead=(k,)) * w[:, k:k + 1]
    out = x_ref[0] + mod_ref[0][5:6] * y
    if final_norm:
        out = out * lax.rsqrt(jnp.mean(out * out, axis=-1, keepdims=True) + EPS) * fg_ref[...]
    o_ref[0] = out


def _combine(dest_flat, x, h, eo, wts, modv, ws_gate, ws_up, ws_down, final_g, final_norm, tm):
    b, s, d = x.shape
    f = ws_gate.shape[-1]
    nst = s // tm
    tok = pl.BlockSpec((1, tm, d), lambda bi, si: (bi, si, 0))
    tr = tm * TOKEN_TILE
    return pl.pallas_call(
        functools.partial(_combine_kernel, final_norm=final_norm),
        grid=(b, nst),
        in_specs=[
            pl.BlockSpec((tm * TOP_K,), lambda bi, si: (bi * nst + si,), memory_space=pltpu.SMEM),
            tok,
            pl.BlockSpec((tr, TOKEN_W), lambda bi, si: (bi * nst + si, 0)),
            pl.BlockSpec(memory_space=pl.ANY),
            pl.BlockSpec((tm, TOP_K), lambda bi, si: (bi * nst + si, 0)),
            pl.BlockSpec((1, N_MOD, d), lambda bi, si: (bi, 0, 0)),
            _const_spec((d, f)), _const_spec((d, f)), _const_spec((f, d)),
            _const_spec((1, d)),
        ],
        out_specs=tok,
        out_shape=jax.ShapeDtypeStruct((b, s, d), F32),
        scratch_shapes=[pltpu.VMEM((TOP_K, tr, TOKEN_W), F32), pltpu.SemaphoreType.DMA(())],
        compiler_params=_params("arbitrary", "arbitrary"),
        name="moe_combine",
    )(dest_flat, x, h, eo, wts, modv, ws_gate.astype(BF16), ws_up.astype(BF16), ws_down.astype(BF16),
      final_g.reshape(1, d))


def _moe_layer(x, modv, norm_g, w_router, router_bias, w_gate, w_up, w_down, layer,
               ws_gate, ws_up, ws_down, final_g, final_norm, mixer_out=None):
    b, s, d = x.shape
    t = b * s
    nb = pl.cdiv(t * TOP_K, MOE_ROWS) + N_EXPERTS
    tile = _token_tile(s, 512)
    routed = _router(x, modv, norm_g, w_router, router_bias, tile, mixer_out)
    h, rank, eid, w6, cnt = routed[:5]
    if mixer_out is not None:
        x = routed[5]
    counts = cnt[:, 0].astype(jnp.int32)
    dest_flat, block_e, pad_rows = _dispatch_plan(rank[:TOP_K], eid[:TOP_K], counts, nb, tile)
    wts = w6[:TOP_K].T
    xs = _dispatch(dest_flat, pad_rows, h, nb * MOE_ROWS, tile)
    eo = _experts(xs, block_e, w_gate, w_up, w_down, layer)
    return _combine(dest_flat, x, h, eo, wts, modv, ws_gate, ws_up, ws_down, final_g, final_norm, tile)


def kernel(x, c, positions, norm1_g, norm2_g, ada_w, ada_b, da_w_in, da_lam_q1, da_lam_k1, da_lam_q2, da_lam_k2, da_subln_g, da_w_out, sg_w_in, sg_ln_g, sg_ln_b, sg_w_s, sg_b_s, sg_w_out, moe_w_router, moe_router_bias, moe_w_gate, moe_w_up, moe_w_down, moe_ws_gate, moe_ws_up, moe_ws_down, final_g):
    depth = ada_w.shape[0]
    mod = _ada_mod(c, ada_w, ada_b)
    for i in range(depth):
        j = i // 2
        if i % 2 == 0:
            lambda_init = 0.8 - 0.6 * math.exp(-0.3 * i)
            q, k, v = _qkv_rope(x, mod[i], norm1_g[i], positions, da_w_in[j].astype(BF16))
            lam_params = jnp.stack([da_lam_q1[j], da_lam_k1[j], da_lam_q2[j], da_lam_k2[j]])
            o = _diff_attention(q, k, v, lam_params, da_subln_g[j], lambda_init)
            mixer_out = (o, da_w_out[j].astype(BF16))
        else:
            x = _spatial_gating(x, mod[i], norm1_g[i], sg_w_in[j], sg_ln_g[j], sg_ln_b[j],
                                sg_w_s[j], sg_b_s[j], sg_w_out[j])
            mixer_out = None
        x = _moe_layer(x, mod[i], norm2_g[i], moe_w_router[i], moe_router_bias[i],
                       moe_w_gate, moe_w_up, moe_w_down, i,
                       moe_ws_gate[i], moe_ws_up[i], moe_ws_down[i],
                       final_g, final_norm=(i == depth - 1), mixer_out=mixer_out)
    return x
```

```python
import functools
import math

import jax
import jax.numpy as jnp
from jax import lax
from jax.experimental import pallas as pl
from jax.experimental.pallas import tpu as pltpu

F32 = jnp.float32
BF16 = jnp.bfloat16
HIGHEST = lax.Precision.HIGHEST

EPS = 1e-6
LANES = 128
N_EXPERTS = 64
TOP_K = 6
N_EXPERT_GROUPS = 8
TOPK_GROUPS = 4
GROUP_SIZE = N_EXPERTS // N_EXPERT_GROUPS
ROUTED_SCALE = 2.5
DA_HEAD_DIM = 64
ROPE_THETA = 10000.0
SG_CHUNK = 128
SG_GROUPS = 8
N_MOD = 8

MOE_ROWS = 512
TOKEN_W = 128
TOKEN_TILE = 8
VMEM_LIMIT = 56 * 1024 * 1024


def _params(*sem):
    return pltpu.CompilerParams(dimension_semantics=sem, vmem_limit_bytes=VMEM_LIMIT)


def _const_spec(shape):
    n = len(shape)
    return pl.BlockSpec(shape, lambda *_: (0,) * n, pipeline_mode=pl.Buffered(1))


def _token_tile(s, target):
    t = min(s, target)
    assert s % t == 0
    return t


def _rows_to_tiles(ref, val):
    n, d = val.shape
    per = d // TOKEN_W
    for i in range(per):
        ref[pl.ds(i, n, stride=per), :] = val[:, i * TOKEN_W:(i + 1) * TOKEN_W]


def _rows_from_tiles(ref, n, per, lead=()):
    return jnp.concatenate([ref[lead + (pl.ds(i, n, stride=per), slice(None))] for i in range(per)],
                           axis=-1)


def _norm_mod(x, g, shift, scale):
    y = x * lax.rsqrt(jnp.mean(x * x, axis=-1, keepdims=True) + EPS) * g
    return y * (1.0 + scale) + shift


def _mod_kernel(c_ref, w_ref, b_ref, o_ref):
    c = c_ref[...]
    cond = c * jax.nn.sigmoid(c)
    o_ref[0] = jnp.dot(cond, w_ref[0], preferred_element_type=F32, precision=HIGHEST) + b_ref[0]


def _ada_mod(c, ada_w, ada_b):
    depth, d, d6 = ada_w.shape
    b = c.shape[0]
    nj = d6 // d
    out = pl.pallas_call(
        _mod_kernel,
        grid=(depth, nj),
        in_specs=[
            pl.BlockSpec((b, d), lambda i, j: (0, 0)),
            pl.BlockSpec((1, d, d), lambda i, j: (i, 0, j)),
            pl.BlockSpec((1, 1, d), lambda i, j: (i, 0, j)),
        ],
        out_specs=pl.BlockSpec((1, b, d), lambda i, j: (i, 0, j)),
        out_shape=jax.ShapeDtypeStruct((depth, b, d6), F32),
        compiler_params=_params("arbitrary", "arbitrary"),
        name="ada_mod",
    )(c, ada_w, ada_b.reshape(depth, 1, d6))
    mod = out.reshape(depth, b, nj, d)
    return jnp.pad(mod, ((0, 0), (0, 0), (0, N_MOD - nj), (0, 0)))


def _qkv_kernel(x_ref, mod_ref, g_ref, pos_ref, freq_ref, w_ref, q_ref, k_ref, v_ref):
    x = x_ref[0]
    mod = mod_ref[0]
    d = x.shape[-1]
    h = _norm_mod(x, g_ref[...], mod[0:1], mod[1:2])
    qkv = jnp.dot(h.astype(BF16), w_ref[...], preferred_element_type=F32)
    ang = pos_ref[0].astype(F32) * freq_ref[...]
    cos = jnp.cos(ang)
    sin = jnp.sin(ang)
    lane = lax.broadcasted_iota(jnp.int32, ang.shape, 1)
    first_half = (lane % DA_HEAD_DIM) < (DA_HEAD_DIM // 2)
    sin_signed = jnp.where(first_half, -sin, sin)
    half = DA_HEAD_DIM // 2

    def rope(blk):
        partner = jnp.where(first_half, pltpu.roll(blk, LANES - half, 1), pltpu.roll(blk, half, 1))
        return blk * cos + partner * sin_signed

    q_scale = DA_HEAD_DIM ** -0.5 * math.log2(math.e)
    for cb in range(d // LANES):
        lo = cb * LANES
        q_ref[0, :, lo:lo + LANES] = (rope(qkv[:, lo:lo + LANES]) * q_scale).astype(BF16)
        k_ref[0, :, lo:lo + LANES] = rope(qkv[:, d + lo:d + lo + LANES]).astype(BF16)
    v_ref[0] = qkv[:, 2 * d:].astype(BF16)


def _qkv_rope(x, modv, norm_g, positions, w_in_bf16):
    b, s, d = x.shape
    tm = _token_tile(s, 512)
    inv_freq = ROPE_THETA ** (-jnp.arange(0, DA_HEAD_DIM, 2, dtype=F32) / DA_HEAD_DIM)
    freq = jnp.tile(inv_freq, LANES // (DA_HEAD_DIM // 2)).reshape(1, LANES)
    tok = pl.BlockSpec((1, tm, d), lambda bi, si: (bi, si, 0))
    out = jax.ShapeDtypeStruct((b, s, d), BF16)
    return pl.pallas_call(
        _qkv_kernel,
        grid=(b, s // tm),
        in_specs=[
            tok,
            pl.BlockSpec((1, N_MOD, d), lambda bi, si: (bi, 0, 0)),
            _const_spec((1, d)),
            pl.BlockSpec((1, tm, 1), lambda bi, si: (bi, si, 0)),
            _const_spec((1, LANES)),
            _const_spec((d, 3 * d)),
        ],
        out_specs=[tok, tok, tok],
        out_shape=[out, out, out],
        compiler_params=_params("parallel", "parallel"),
        name="qkv_rope",
    )(x, modv, norm_g.reshape(1, d), positions.reshape(b, s, 1), freq, w_in_bf16)


ATTN_ROW_CHUNK = 64


ATTN_NORM_CHUNK = 1024
ATTN_SAFE_BOUND = 50.0


def _subhead_sq_norms(x):
    xf = x.astype(F32)
    sq = (xf * xf).astype(BF16)
    row = lax.broadcasted_iota(jnp.int32, (LANES, LANES), 0)
    first = (row < DA_HEAD_DIM).astype(BF16)
    second = (row >= DA_HEAD_DIM).astype(BF16)
    return (jnp.dot(sq, first, preferred_element_type=F32),
            jnp.dot(sq, second, preferred_element_type=F32))


def _attn_kernel(lam_ref, q_ref, k_ref, v_ref, g_ref, o_ref, qs_ref, s_ref, p_ref,
                 m_ref, l_ref, acc_ref, kmax_ref, *, lambda_init, tq, tk):
    rows = 2 * tq
    n_tiles = k_ref.shape[1] // tk

    @pl.when(pl.program_id(2) == 0)
    def _():
        chunk = min(ATTN_NORM_CHUNK, k_ref.shape[1])

        def body(c, carry):
            off = pl.multiple_of(c * chunk, chunk)
            n1, n2 = _subhead_sq_norms(k_ref[0, pl.ds(off, chunk), :])
            return (jnp.maximum(carry[0], jnp.max(n1, axis=0, keepdims=True)),
                    jnp.maximum(carry[1], jnp.max(n2, axis=0, keepdims=True)))

        zeros = jnp.zeros((1, LANES), F32)
        k1, k2 = lax.fori_loop(0, k_ref.shape[1] // chunk, body, (zeros, zeros))
        kmax_ref[0:1] = k1
        kmax_ref[1:2] = k2

    q = q_ref[0]
    lane = lax.broadcasted_iota(jnp.int32, q.shape, 1)
    zero = jnp.zeros_like(q)
    qs_ref[0:tq] = jnp.where(lane < DA_HEAD_DIM, q, zero)
    qs_ref[tq:rows] = jnp.where(lane >= DA_HEAD_DIM, q, zero)
    l_ref[...] = jnp.zeros(l_ref.shape, F32)
    acc_ref[...] = jnp.zeros(acc_ref.shape, F32)
    qn1, qn2 = _subhead_sq_norms(q)
    bound1 = jnp.sqrt(jnp.max(qn1, axis=0, keepdims=True) * kmax_ref[0:1])
    bound2 = jnp.sqrt(jnp.max(qn2, axis=0, keepdims=True) * kmax_ref[1:2])
    fixed_shift = jnp.max(jnp.maximum(bound1, bound2)) <= ATTN_SAFE_BOUND

    def scores(tile):
        off = pl.multiple_of(tile * tk, tk)
        return lax.dot_general(qs_ref[...], k_ref[0, pl.ds(off, tk), :], (((1,), (1,)), ((), ())),
                               preferred_element_type=F32)

    def add_pv(tile):
        off = pl.multiple_of(tile * tk, tk)
        acc_ref[...] += jnp.dot(p_ref[...], v_ref[0, pl.ds(off, tk), :], preferred_element_type=F32)

    @pl.when(fixed_shift)
    def _():
        def step(tile, carry):
            s = scores(tile)
            for r0 in range(0, rows, ATTN_ROW_CHUNK):
                rs = slice(r0, r0 + ATTN_ROW_CHUNK)
                m = bound1 if r0 < tq else bound2
                psum = None
                for c0 in range(0, tk, LANES):
                    p = jnp.exp2(s[rs, c0:c0 + LANES] - m)
                    psum = p if psum is None else psum + p
                    p_ref[rs, c0:c0 + LANES] = p.astype(BF16)
                l_ref[rs] += psum
            add_pv(tile)
            return carry

        lax.fori_loop(0, n_tiles, step, 0)

    @pl.when(jnp.logical_not(fixed_shift))
    def _():
        m_ref[...] = jnp.full(m_ref.shape, -jnp.inf, F32)

        def step(tile, carry):
            s_ref[...] = scores(tile)
            for r0 in range(0, rows, ATTN_ROW_CHUNK):
                rs = slice(r0, r0 + ATTN_ROW_CHUNK)
                mx = s_ref[rs, 0:LANES]
                for c0 in range(LANES, tk, LANES):
                    mx = jnp.maximum(mx, s_ref[rs, c0:c0 + LANES])
                m_prev = m_ref[rs]
                m_new = jnp.maximum(m_prev, jnp.max(mx, axis=-1, keepdims=True))
                alpha = jnp.exp2(m_prev - m_new)
                psum = None
                for c0 in range(0, tk, LANES):
                    p = jnp.exp2(s_ref[rs, c0:c0 + LANES] - m_new)
                    psum = p if psum is None else psum + p
                    p_ref[rs, c0:c0 + LANES] = p.astype(BF16)
                l_ref[rs] = alpha * l_ref[rs] + psum
                m_ref[rs] = m_new
                acc_ref[rs] = alpha * acc_ref[rs]
            add_pv(tile)
            return carry

        lax.fori_loop(0, n_tiles, step, 0)

    o = acc_ref[...] / jnp.sum(l_ref[...], axis=-1, keepdims=True)
    lp = lam_ref[...]
    lam = (jnp.exp(jnp.sum(lp[0:1] * lp[1:2], axis=-1, keepdims=True))
           - jnp.exp(jnp.sum(lp[2:3] * lp[3:4], axis=-1, keepdims=True)) + lambda_init)
    diff = o[0:tq] - lam * o[tq:2 * tq]
    y = diff * lax.rsqrt(jnp.mean(diff * diff, axis=-1, keepdims=True) + EPS) * g_ref[...]
    o_ref[0] = (y * (1.0 - lambda_init)).astype(BF16)


def _diff_attention(q, k, v, lam_params, subln_g, lambda_init):
    b, s, d = q.shape
    hw = 2 * DA_HEAD_DIM
    nh = d // hw
    tq = _token_tile(s, 1024)
    tk = _token_tile(s, 2048)
    kern = functools.partial(_attn_kernel, lambda_init=lambda_init, tq=tq, tk=tk)
    return pl.pallas_call(
        kern,
        grid=(b, nh, s // tq),
        in_specs=[
            _const_spec((4, DA_HEAD_DIM)),
            pl.BlockSpec((1, tq, hw), lambda bi, hi, qi: (bi, qi, hi)),
            pl.BlockSpec((1, s, hw), lambda bi, hi, qi: (bi, 0, hi)),
            pl.BlockSpec((1, s, hw), lambda bi, hi, qi: (bi, 0, hi)),
            _const_spec((1, hw)),
        ],
        out_specs=pl.BlockSpec((1, tq, hw), lambda bi, hi, qi: (bi, qi, hi)),
        out_shape=jax.ShapeDtypeStruct((b, s, d), BF16),
        scratch_shapes=[
            pltpu.VMEM((2 * tq, hw), BF16),
            pltpu.VMEM((2 * tq, tk), F32),
            pltpu.VMEM((2 * tq, tk), BF16),
            pltpu.VMEM((2 * tq, LANES), F32),
            pltpu.VMEM((2 * tq, LANES), F32),
            pltpu.VMEM((2 * tq, hw), F32),
            pltpu.VMEM((8, LANES), F32),
        ],
        compiler_params=_params("parallel", "parallel", "arbitrary"),
        name="diff_attn",
    )(lam_params, q, k, v, subln_g.reshape(1, hw))


def _sg_kernel(x_ref, mod_ref, g_ref, win_ref, lng_ref, lnb_ref, ws_ref, bs_ref, wout_ref,
               x1_ref, gated_ref):
    x = x_ref[0]
    mod = mod_ref[0]
    tm = x.shape[0]
    half = lng_ref.shape[-1]
    gdim = half // SG_GROUPS
    h = _norm_mod(x, g_ref[...], mod[0:1], mod[1:2])
    z = jnp.dot(h.astype(BF16), win_ref[...], preferred_element_type=F32)
    z = 0.5 * z * (1.0 + lax.erf(z * (2.0 ** -0.5)))
    u = z[:, :half]
    v = z[:, half:]
    mu = jnp.mean(v, axis=-1, keepdims=True)
    vc = v - mu
    v = vc * lax.rsqrt(jnp.mean(vc * vc, axis=-1, keepdims=True) + EPS) * lng_ref[...] + lnb_ref[...]
    vb = v.astype(BF16)
    for c in range(tm // SG_CHUNK):
        r0 = c * SG_CHUNK
        for g in range(SG_GROUPS):
            c0 = g * gdim
            sp = jnp.dot(ws_ref[g], vb[r0:r0 + SG_CHUNK, c0:c0 + gdim],
                         preferred_element_type=F32) + bs_ref[g]
            gated_ref[r0:r0 + SG_CHUNK, c0:c0 + gdim] = (
                u[r0:r0 + SG_CHUNK, c0:c0 + gdim] * sp).astype(BF16)
    mix = jnp.dot(gated_ref[...], wout_ref[...], preferred_element_type=F32)
    x1_ref[0] = x + mod[2:3] * mix


def _spatial_gating(x, modv, norm_g, w_in, ln_g, ln_b, w_s, b_s, w_out):
    b, s, d = x.shape
    half = ln_g.shape[-1]
    tm = _token_tile(s, 512)
    assert tm % SG_CHUNK == 0
    tok = pl.BlockSpec((1, tm, d), lambda bi, si: (bi, si, 0))
    return pl.pallas_call(
        _sg_kernel,
        grid=(b, s // tm),
        in_specs=[
            tok,
            pl.BlockSpec((1, N_MOD, d), lambda bi, si: (bi, 0, 0)),
            _const_spec((1, d)),
            _const_spec((d, 2 * half)),
            _const_spec((1, half)),
            _const_spec((1, half)),
            _const_spec((SG_GROUPS, SG_CHUNK, SG_CHUNK)),
            _const_spec((SG_GROUPS, SG_CHUNK, 1)),
            _const_spec((half, d)),
        ],
        out_specs=tok,
        out_shape=jax.ShapeDtypeStruct((b, s, d), F32),
        scratch_shapes=[pltpu.VMEM((tm, half), BF16)],
        compiler_params=_params("parallel", "parallel"),
        name="spatial_gating",
    )(x, modv, norm_g.reshape(1, d), w_in.astype(BF16), ln_g.reshape(1, half), ln_b.reshape(1, half),
      w_s.astype(BF16), b_s.reshape(SG_GROUPS, SG_CHUNK, 1), w_out.astype(BF16))


def _first_argmax(cur, idx, sentinel):
    m = jnp.max(cur, axis=0, keepdims=True)
    first = jnp.min(jnp.where(cur == m, idx, sentinel), axis=0, keepdims=True)
    return m, idx == first


def _router_kernel(x_ref, mod_ref, g_ref, wrt_ref, bias_ref, upper_ref, lower_ref, *rest, mixer_out):
    mod = mod_ref[0]
    x = x_ref[0]
    if mixer_out:
        o_ref, wout_ref, h_ref, rank_ref, eid_ref, w_ref, cnt_ref, x1_ref = rest
        x = x + mod[2:3] * jnp.dot(o_ref[0], wout_ref[...], preferred_element_type=F32)
        x1_ref[0] = x
    else:
        h_ref, rank_ref, eid_ref, w_ref, cnt_ref = rest
    h = _norm_mod(x, g_ref[...], mod[3:4], mod[4:5])
    _rows_to_tiles(h_ref, h)
    logits = lax.dot_general(wrt_ref[...], h, (((1,), (1,)), ((), ())),
                             precision=HIGHEST, preferred_element_type=F32)
    scores = jax.nn.sigmoid(logits)
    sel = scores + bias_ref[...]
    tm = sel.shape[1]
    neg = -jnp.inf
    sub = lax.broadcasted_iota(jnp.int32, (GROUP_SIZE, tm), 0)
    rows = []
    for g in range(N_EXPERT_GROUPS):
        blk = sel[g * GROUP_SIZE:(g + 1) * GROUP_SIZE]
        m1, hit = _first_argmax(blk, sub, GROUP_SIZE)
        m2 = jnp.max(jnp.where(hit, neg, blk), axis=0, keepdims=True)
        rows.append(m1 + m2)
    cur = jnp.concatenate(rows, axis=0)
    gsel = jnp.zeros(cur.shape, jnp.int32)
    for _ in range(TOPK_GROUPS):
        _, hit = _first_argmax(cur, sub, N_EXPERT_GROUPS)
        gsel = jnp.where(hit, 1, gsel)
        cur = jnp.where(hit, neg, cur)
    masked = []
    for g in range(N_EXPERT_GROUPS):
        blk = sel[g * GROUP_SIZE:(g + 1) * GROUP_SIZE]
        masked.append(jnp.where(gsel[g:g + 1] > 0, blk, neg))
    cur = jnp.concatenate(masked, axis=0)
    eidx = lax.broadcasted_iota(jnp.int32, cur.shape, 0)
    chosen = jnp.zeros(cur.shape, jnp.int32)
    for _ in range(TOP_K):
        _, hit = _first_argmax(cur, eidx, N_EXPERTS)
        chosen = jnp.where(hit, 1, chosen)
        cur = jnp.where(hit, neg, cur)
    picked = jnp.where(chosen > 0, scores, 0.0)
    wsum = jnp.sum(picked, axis=0, keepdims=True)
    wdense = picked / wsum * ROUTED_SCALE

    @pl.when((pl.program_id(0) == 0) & (pl.program_id(1) == 0))
    def _():
        cnt_ref[...] = jnp.zeros(cnt_ref.shape, F32)

    cm = chosen.astype(BF16)
    base = cnt_ref[...]
    rank = (jnp.dot(cm, upper_ref[...], preferred_element_type=F32)
            + jnp.concatenate([base] * (tm // LANES), axis=1))
    cnt_ref[...] = base + jnp.dot(cm, jnp.ones((tm, LANES), BF16), preferred_element_type=F32)
    ordinal = jnp.dot(lower_ref[...], cm, preferred_element_type=F32)
    eid = eidx.astype(F32)
    rows_r, rows_e, rows_w = [], [], []
    for k in range(TOP_K):
        pick = (chosen > 0) & (ordinal == k)
        rows_r.append(jnp.sum(jnp.where(pick, rank, 0.0), axis=0, keepdims=True))
        rows_e.append(jnp.sum(jnp.where(pick, eid, 0.0), axis=0, keepdims=True))
        rows_w.append(jnp.sum(jnp.where(pick, wdense, 0.0), axis=0, keepdims=True))
    fill = [jnp.zeros((N_MOD - TOP_K, tm), F32)]
    rank_ref[...] = jnp.concatenate(rows_r + fill, axis=0).astype(jnp.int32)
    eid_ref[...] = jnp.concatenate(rows_e + fill, axis=0).astype(jnp.int32)
    w_ref[...] = jnp.concatenate(rows_w + fill, axis=0)


def _router(x, modv, norm_g, w_router, router_bias, tm, mixer_out=None):
    b, s, d = x.shape
    t = b * s
    e = w_router.shape[1]
    nst = s // tm
    tok = pl.BlockSpec((1, tm, d), lambda bi, si: (bi, si, 0))
    per_k = pl.BlockSpec((N_MOD, tm), lambda bi, si: (0, bi * nst + si))
    upper = (jnp.arange(tm)[:, None] < jnp.arange(tm)[None, :]).astype(BF16)
    lower = (jnp.arange(e)[None, :] < jnp.arange(e)[:, None]).astype(BF16)
    in_specs = [
        tok,
        pl.BlockSpec((1, N_MOD, d), lambda bi, si: (bi, 0, 0)),
        _const_spec((1, d)),
        _const_spec((e, d)),
        _const_spec((e, 1)),
        _const_spec((tm, tm)),
        _const_spec((e, e)),
    ]
    args = [x, modv, norm_g.reshape(1, d), w_router.T, router_bias.reshape(e, 1), upper, lower]
    out_specs = [pl.BlockSpec((tm * TOKEN_TILE, TOKEN_W), lambda bi, si: (bi * nst + si, 0)),
                 per_k, per_k, per_k,
                 pl.BlockSpec((e, LANES), lambda bi, si: (0, 0))]
    out_shape = [jax.ShapeDtypeStruct((t * TOKEN_TILE, TOKEN_W), F32),
                 jax.ShapeDtypeStruct((N_MOD, t), jnp.int32),
                 jax.ShapeDtypeStruct((N_MOD, t), jnp.int32),
                 jax.ShapeDtypeStruct((N_MOD, t), F32),
                 jax.ShapeDtypeStruct((e, LANES), F32)]
    if mixer_out is not None:
        in_specs += [tok, _const_spec((d, d))]
        args += list(mixer_out)
        out_specs.append(tok)
        out_shape.append(jax.ShapeDtypeStruct((b, s, d), F32))
    return pl.pallas_call(
        functools.partial(_router_kernel, mixer_out=mixer_out is not None),
        grid=(b, nst),
        in_specs=in_specs,
        out_specs=out_specs,
        out_shape=out_shape,
        compiler_params=_params("arbitrary", "arbitrary"),
        name="moe_router",
    )(*args)


def _dispatch_plan(rank, eid, counts, n_blocks, tile):
    e = counts.shape[0]
    t = rank.shape[1]
    padded = (counts + MOE_ROWS - 1) // MOE_ROWS * MOE_ROWS
    pad_end = jnp.cumsum(padded)
    pad_start = pad_end - padded
    start_of = jnp.sum(jnp.where(eid[None] == jnp.arange(e)[:, None, None], pad_start[:, None, None], 0),
                       axis=0)
    dest = (rank + start_of).astype(jnp.int32)
    dest = dest.reshape(TOP_K, t // tile, tile).transpose(1, 0, 2).reshape(t * TOP_K)
    blk_start = jnp.arange(n_blocks, dtype=jnp.int32) * MOE_ROWS
    block_e = jnp.minimum(jnp.sum(pad_end[None, :] <= blk_start[:, None], axis=1), e - 1).astype(jnp.int32)
    slack = padded - counts
    slack_end = jnp.cumsum(slack)
    q = jnp.arange(n_blocks * MOE_ROWS - t * TOP_K, dtype=jnp.int32)
    eq = jnp.sum(slack_end[None, :] <= q[:, None], axis=1)
    first_free = pad_start + counts - (slack_end - slack)
    in_expert = q + jnp.sum(jnp.where(eq[None] == jnp.arange(e)[:, None], first_free[:, None], 0), axis=0)
    pad_rows = jnp.where(eq < e, in_expert, pad_end[-1] + q - slack_end[-1]).astype(jnp.int32)
    return dest, block_e, pad_rows


def _row_copy(src, dst, s_tok, d_tok, sem):
    s0 = pl.multiple_of(s_tok * TOKEN_TILE, TOKEN_TILE)
    d0 = pl.multiple_of(d_tok * TOKEN_TILE, TOKEN_TILE)
    return pltpu.make_async_copy(src.at[pl.ds(s0, TOKEN_TILE)], dst.at[pl.ds(d0, TOKEN_TILE)], sem)


def _drain(src, dst, sem, tokens, repeats):
    n = tokens * TOKEN_TILE
    for _ in range(repeats):
        pltpu.make_async_copy(src.at[pl.ds(0, n)], dst.at[pl.ds(0, n)], sem).wait()


def _dispatch_kernel(pads_ref, dest_ref, h_ref, xs_ref, zero_ref, sem, *, tokens, pads):
    def body(i, carry):
        for k in range(TOP_K):
            _row_copy(h_ref, xs_ref, i, dest_ref[k * tokens + i], sem).start(priority=k % 2)
        return carry

    lax.fori_loop(0, tokens, body, 0, unroll=4)
    zero_ref[...] = jnp.zeros(zero_ref.shape, zero_ref.dtype)
    pad0 = pl.program_id(0) * pads

    def fill(i, carry):
        _row_copy(zero_ref, xs_ref, 0, pads_ref[pad0 + i], sem).start()
        return carry

    lax.fori_loop(0, pads, fill, 0)
    _drain(h_ref, xs_ref, sem, tokens, TOP_K + pads // tokens)
    if pads % tokens:
        _drain(h_ref, xs_ref, sem, pads % tokens, 1)


def _dispatch(dest_flat, pad_rows, h, out_tokens, tokens):
    t = h.shape[0] // TOKEN_TILE
    steps = t // tokens
    pads = pad_rows.shape[0] // steps
    assert pads * steps == pad_rows.shape[0]
    return pl.pallas_call(
        functools.partial(_dispatch_kernel, tokens=tokens, pads=pads),
        grid_spec=pltpu.PrefetchScalarGridSpec(
            num_scalar_prefetch=1,
            grid=(steps,),
            in_specs=[
                pl.BlockSpec((tokens * TOP_K,), lambda i, pr: (i,), memory_space=pltpu.SMEM),
                pl.BlockSpec((tokens * TOKEN_TILE, TOKEN_W), lambda i, pr: (i, 0)),
            ],
            out_specs=pl.BlockSpec(memory_space=pl.ANY),
            scratch_shapes=[pltpu.VMEM((TOKEN_TILE, TOKEN_W), h.dtype), pltpu.SemaphoreType.DMA(())],
        ),
        out_shape=jax.ShapeDtypeStruct((out_tokens * TOKEN_TILE, TOKEN_W), h.dtype),
        compiler_params=_params("arbitrary"),
        name="moe_dispatch",
    )(pad_rows, dest_flat, h)


def _expert_kernel(be_ref, xs_ref, wg_ref, wu_ref, wd_ref, o_ref, wgu_s, wd_s):
    i = pl.program_id(0)
    f = wd_s.shape[0]

    @pl.when((i == 0) | (be_ref[i] != be_ref[jnp.maximum(i - 1, 0)]))
    def _():
        wgu_s[:, 0:f] = wg_ref[0].astype(BF16)
        wgu_s[:, f:2 * f] = wu_ref[0].astype(BF16)
        wd_s[...] = wd_ref[0].astype(BF16)

    xb = _rows_from_tiles(xs_ref, MOE_ROWS, TOKEN_TILE).astype(BF16)
    gu = jnp.dot(xb, wgu_s[...], preferred_element_type=F32)
    gate = gu[:, 0:f]
    up = gu[:, f:2 * f]
    hb = (gate * jax.nn.sigmoid(gate) * up).astype(BF16)
    _rows_to_tiles(o_ref, jnp.dot(hb, wd_s[...], preferred_element_type=F32))


def _experts(xs, block_e, w_gate, w_up, w_down, layer):
    d, f = w_gate.shape[2:]
    assert d == TOKEN_TILE * TOKEN_W
    nb = xs.shape[0] // (MOE_ROWS * TOKEN_TILE)
    rows = pl.BlockSpec((MOE_ROWS * TOKEN_TILE, TOKEN_W), lambda i, be: (i, 0))
    return pl.pallas_call(
        _expert_kernel,
        grid_spec=pltpu.PrefetchScalarGridSpec(
            num_scalar_prefetch=1,
            grid=(nb,),
            in_specs=[
                rows,
                pl.BlockSpec((None, 1, d, f), lambda i, be: (layer, be[i], 0, 0)),
                pl.BlockSpec((None, 1, d, f), lambda i, be: (layer, be[i], 0, 0)),
                pl.BlockSpec((None, 1, f, d), lambda i, be: (layer, be[i], 0, 0)),
            ],
            out_specs=rows,
            scratch_shapes=[pltpu.VMEM((d, 2 * f), BF16), pltpu.VMEM((f, d), BF16)],
        ),
        out_shape=jax.ShapeDtypeStruct(xs.shape, F32),
        compiler_params=_params("arbitrary"),
        name="moe_experts",
    )(block_e, xs, w_gate, w_up, w_down)


def _combine_kernel(dest_ref, x_ref, h_ref, eo_ref, w_ref, mod_ref, sg_ref, su_ref, sd_ref, fg_ref, o_ref,
                    buf_ref, sem, *, final_norm):
    tm = x_ref.shape[1]

    def gather(i, carry):
        for k in range(TOP_K):
            _row_copy(eo_ref, buf_ref.at[k], dest_ref[k * tm + i], i, sem).start(priority=k % 2)
        return carry

    lax.fori_loop(0, tm, gather, 0, unroll=4)
    hb = _rows_from_tiles(h_ref, tm, TOKEN_TILE).astype(BF16)
    gate = jnp.dot(hb, sg_ref[...], preferred_element_type=F32)
    up = jnp.dot(hb, su_ref[...], preferred_element_type=F32)
    y = jnp.dot((gate * jax.nn.sigmoid(gate) * up).astype(BF16), sd_ref[...], preferred_element_type=F32)
    w = w_ref[...]
    _drain(eo_ref, buf_ref.at[0], sem, tm, TOP_K)
    for k in range(TOP_K):
        y = y + _rows_from_tiles(buf_ref, tm, TOKEN_TILE, lead=(k,)) * w[:, k:k + 1]
    out = x_ref[0] + mod_ref[0][5:6] * y
    if final_norm:
        out = out * lax.rsqrt(jnp.mean(out * out, axis=-1, keepdims=True) + EPS) * fg_ref[...]
    o_ref[0] = out


def _combine(dest_flat, x, h, eo, wts, modv, ws_gate, ws_up, ws_down, final_g, final_norm, tm):
    b, s, d = x.shape
    f = ws_gate.shape[-1]
    nst = s // tm
    tok = pl.BlockSpec((1, tm, d), lambda bi, si: (bi, si, 0))
    tr = tm * TOKEN_TILE
    return pl.pallas_call(
        functools.partial(_combine_kernel, final_norm=final_norm),
        grid=(b, nst),
        in_specs=[
            pl.BlockSpec((tm * TOP_K,), lambda bi, si: (bi * nst + si,), memory_space=pltpu.SMEM),
            tok,
            pl.BlockSpec((tr, TOKEN_W), lambda bi, si: (bi * nst + si, 0)),
            pl.BlockSpec(memory_space=pl.ANY),
            pl.BlockSpec((tm, TOP_K), lambda bi, si: (bi * nst + si, 0)),
            pl.BlockSpec((1, N_MOD, d), lambda bi, si: (bi, 0, 0)),
            _const_spec((d, f)), _const_spec((d, f)), _const_spec((f, d)),
            _const_spec((1, d)),
        ],
        out_specs=tok,
        out_shape=jax.ShapeDtypeStruct((b, s, d), F32),
        scratch_shapes=[pltpu.VMEM((TOP_K, tr, TOKEN_W), F32), pltpu.SemaphoreType.DMA(())],
        compiler_params=_params("arbitrary", "arbitrary"),
        name="moe_combine",
    )(dest_flat, x, h, eo, wts, modv, ws_gate.astype(BF16), ws_up.astype(BF16), ws_down.astype(BF16),
      final_g.reshape(1, d))


def _moe_layer(x, modv, norm_g, w_router, router_bias, w_gate, w_up, w_down, layer,
               ws_gate, ws_up, ws_down, final_g, final_norm, mixer_out=None):
    b, s, d = x.shape
    t = b * s
    nb = pl.cdiv(t * TOP_K, MOE_ROWS) + N_EXPERTS
    tile = _token_tile(s, 512)
    routed = _router(x, modv, norm_g, w_router, router_bias, tile, mixer_out)
    h, rank, eid, w6, cnt = routed[:5]
    if mixer_out is not None:
        x = routed[5]
    counts = cnt[:, 0].astype(jnp.int32)
    dest_flat, block_e, pad_rows = _dispatch_plan(rank[:TOP_K], eid[:TOP_K], counts, nb, tile)
    wts = w6[:TOP_K].T
    xs = _dispatch(dest_flat, pad_rows, h, nb * MOE_ROWS, tile)
    eo = _experts(xs, block_e, w_gate, w_up, w_down, layer)
    return _combine(dest_flat, x, h, eo, wts, modv, ws_gate, ws_up, ws_down, final_g, final_norm, tile)


def kernel(x, c, positions, norm1_g, norm2_g, ada_w, ada_b, da_w_in, da_lam_q1, da_lam_k1, da_lam_q2, da_lam_k2, da_subln_g, da_w_out, sg_w_in, sg_ln_g, sg_ln_b, sg_w_s, sg_b_s, sg_w_out, moe_w_router, moe_router_bias, moe_w_gate, moe_w_up, moe_w_down, moe_ws_gate, moe_ws_up, moe_ws_down, final_g):
    depth = ada_w.shape[0]
    mod = _ada_mod(c, ada_w, ada_b)
    for i in range(depth):
        j = i // 2
        if i % 2 == 0:
            lambda_init = 0.8 - 0.6 * math.exp(-0.3 * i)
            q, k, v = _qkv_rope(x, mod[i], norm1_g[i], positions, da_w_in[j].astype(BF16))
            lam_params = jnp.stack([da_lam_q1[j], da_lam_k1[j], da_lam_q2[j], da_lam_k2[j]])
            o = _diff_attention(q, k, v, lam_params, da_subln_g[j], lambda_init)
            mixer_out = (o, da_w_out[j].astype(BF16))
        else:
            x = _spatial_gating(x, mod[i], norm1_g[i], sg_w_in[j], sg_ln_g[j], sg_ln_b[j],
                                sg_w_s[j], sg_b_s[j], sg_w_out[j])
            mixer_out = None
        x = _moe_layer(x, mod[i], norm2_g[i], moe_w_router[i], moe_router_bias[i],
                       moe_w_gate, moe_w_up, moe_w_down, i,
                       moe_ws_gate[i], moe_ws_up[i], moe_ws_down[i],
                       final_g, final_norm=(i == depth - 1), mixer_out=mixer_out)
    return x
```
